```python
import math
import jax, jax.numpy as jnp
from jax import lax
import numpy as np

D_MODEL = 1024
BATCH = 8
SEQ = 8192
DEPTH = 1
DEC_BATCH = 4
DEC_SEQ = 8192
PAST_LEN = 128

D_HYENA = 512
D_CONV = 512
HYENA_SHORT = 3
CONF_KERNEL = 31
FILTER_EMB = 33
FILTER_BANDS = (FILTER_EMB - 1) // 2
FILTER_HIDDEN = 64
DECAY_TARGET = 1e-2
FAST_DECAY_PCT = 0.3
SLOW_DECAY_PCT = 1.5
D_FF = 2816
N_IN = 3 * D_HYENA + 2 * D_CONV + 2 * D_MODEL
EPS = 1e-6

kernel_name = "hyena_conformer_gated_hybrid_encoder"


def rms_norm(x, g):
    xf = x.astype(jnp.float32)
    y = xf * lax.rsqrt(jnp.mean(xf * xf, axis=-1, keepdims=True) + EPS)
    return (y * g.astype(jnp.float32)).astype(x.dtype)


def layer_norm(x, g, b):
    xf = x.astype(jnp.float32)
    mu = jnp.mean(xf, axis=-1, keepdims=True)
    xc = xf - mu
    y = xc * lax.rsqrt(jnp.mean(xc * xc, axis=-1, keepdims=True) + EPS)
    return (y * g.astype(jnp.float32) + b.astype(jnp.float32)).astype(x.dtype)


def swiglu(x, w_gate, w_up, w_down):
    return (jax.nn.silu(x @ w_gate) * (x @ w_up)) @ w_down


def depthwise_conv(x, w, b):
    K, C = w.shape
    y = lax.conv_general_dilated(
        x, w[:, None, :].astype(x.dtype), window_strides=(1,),
        padding=[(K // 2, K // 2)], dimension_numbers=("NWC", "WIO", "NWC"),
        feature_group_count=C)
    return y + b.astype(x.dtype)


def hyena_filter_spectrum(L, w1, b1, freq1, w2, b2, freq2, w3):
    f32 = jnp.float32
    t = jnp.linspace(0.0, 1.0, L, dtype=f32)[:, None]
    w = (2.0 * math.pi / L) * jnp.arange(L, dtype=f32)[:, None]
    bands = jnp.linspace(1e-4, FILTER_BANDS - 1, FILTER_BANDS, dtype=f32)
    ang = w * bands[None, :]
    z = jnp.concatenate([t, jnp.cos(ang), -jnp.sin(ang)], axis=-1)
    h = jnp.sin(freq1.astype(f32) * (z @ w1.astype(f32) + b1.astype(f32)))
    h = jnp.sin(freq2.astype(f32) * (h @ w2.astype(f32) + b2.astype(f32)))
    h = (h @ w3.astype(f32)).reshape(L, 2, D_HYENA)
    max_decay = math.log(DECAY_TARGET) / FAST_DECAY_PCT
    min_decay = math.log(DECAY_TARGET) / SLOW_DECAY_PCT
    deltas = jnp.linspace(min_decay, max_decay, D_HYENA, dtype=f32)
    h = h * jnp.exp(-t * jnp.abs(deltas))[:, None, :]
    fwd, bwd = h[:, 0], h[:, 1]
    k = jnp.concatenate([fwd[:1] + bwd[:1], fwd[1:], jnp.zeros((1, D_HYENA), f32), bwd[:0:-1]], axis=0)
    k = k * lax.rsqrt(jnp.sum(k * k, axis=0, keepdims=True) + EPS)
    return jnp.fft.rfft(k, axis=0)


def long_conv(v, k_f, bias):
    L = v.shape[1]
    vf32 = v.astype(jnp.float32)
    vf = jnp.fft.rfft(vf32, n=2 * L, axis=1)
    y = jnp.fft.irfft(vf * k_f[None], n=2 * L, axis=1)[:, :L]
    return (y + vf32 * bias.astype(jnp.float32)).astype(v.dtype)


def token_mixer(u, k_f, w_in, hy_short_w, hy_short_b, hy_bias, hy_w_out,
                cv_dw_w, cv_dw_b, cv_ln_g, cv_ln_b, cv_w_out, w_out):
    proj = u @ w_in
    o1 = 3 * D_HYENA
    o2 = o1 + 2 * D_CONV
    hy = depthwise_conv(proj[..., :o1], hy_short_w, hy_short_b)
    cv = proj[..., o1:o2]
    g_a = proj[..., o2:o2 + D_MODEL]
    g_b = proj[..., o2 + D_MODEL:]
    x0 = hy[..., :D_HYENA]
    x1 = hy[..., D_HYENA:2 * D_HYENA]
    v = hy[..., 2 * D_HYENA:]
    v = long_conv(v * x1, k_f, hy_bias)
    y_a = (v * x0) @ hy_w_out
    c = cv[..., :D_CONV] * jax.nn.sigmoid(cv[..., D_CONV:])
    c = depthwise_conv(c, cv_dw_w, cv_dw_b)
    c = jax.nn.silu(layer_norm(c, cv_ln_g, cv_ln_b))
    y_b = c @ cv_w_out
    merged = jax.nn.sigmoid(g_a) * y_a + jax.nn.sigmoid(g_b) * y_b
    return merged @ w_out


def encoder_layer(x, ffn1_norm_pre, ffn1_w_gate, ffn1_w_up, ffn1_w_down, ffn1_norm_post,
                  mix_norm_pre, w_in, hy_short_w, hy_short_b,
                  hy_filt_w1, hy_filt_b1, hy_filt_freq1, hy_filt_w2, hy_filt_b2, hy_filt_freq2, hy_filt_w3,
                  hy_bias, hy_w_out, cv_dw_w, cv_dw_b, cv_ln_g, cv_ln_b, cv_w_out, w_out, mix_norm_post,
                  ffn2_norm_pre, ffn2_w_gate, ffn2_w_up, ffn2_w_down, ffn2_norm_post):
    L = x.shape[1]
    k_f = hyena_filter_spectrum(L, hy_filt_w1, hy_filt_b1, hy_filt_freq1,
                                hy_filt_w2, hy_filt_b2, hy_filt_freq2, hy_filt_w3)
    x = x + 0.5 * rms_norm(swiglu(rms_norm(x, ffn1_norm_pre), ffn1_w_gate, ffn1_w_up, ffn1_w_down), ffn1_norm_post)
    m = token_mixer(rms_norm(x, mix_norm_pre), k_f, w_in, hy_short_w, hy_short_b, hy_bias, hy_w_out,
                    cv_dw_w, cv_dw_b, cv_ln_g, cv_ln_b, cv_w_out, w_out)
    x = x + rms_norm(m, mix_norm_post)
    x = x + 0.5 * rms_norm(swiglu(rms_norm(x, ffn2_norm_pre), ffn2_w_gate, ffn2_w_up, ffn2_w_down), ffn2_norm_post)
    return x


def setup_inputs(seed: int = 0) -> dict:
    key = jax.random.key(seed)
    ks = iter(jax.random.split(key, 48))
    f32 = jnp.float32

    def nrm(shape, scale):
        return jax.random.normal(next(ks), shape, f32) * scale

    def gain(n):
        return 1.0 + 0.1 * jax.random.normal(next(ks), (DEPTH, n), f32)

    d = {}
    d["x_prompt"] = nrm((BATCH, SEQ, D_MODEL), 1.0)
    d["x_sample"] = nrm((DEC_BATCH, DEC_SEQ, D_MODEL), 1.0)
    d["ffn1_norm_pre"] = gain(D_MODEL)
    d["ffn1_w_gate"] = nrm((DEPTH, D_MODEL, D_FF), D_MODEL ** -0.5)
    d["ffn1_w_up"] = nrm((DEPTH, D_MODEL, D_FF), D_MODEL ** -0.5)
    d["ffn1_w_down"] = nrm((DEPTH, D_FF, D_MODEL), D_FF ** -0.5)
    d["ffn1_norm_post"] = gain(D_MODEL)
    d["mix_norm_pre"] = gain(D_MODEL)
    d["w_in"] = nrm((DEPTH, D_MODEL, N_IN), D_MODEL ** -0.5)
    d["hy_short_w"] = nrm((DEPTH, HYENA_SHORT, 3 * D_HYENA), HYENA_SHORT ** -0.5)
    d["hy_short_b"] = nrm((DEPTH, 3 * D_HYENA), 0.02)
    d["hy_filt_w1"] = nrm((DEPTH, FILTER_EMB, FILTER_HIDDEN), FILTER_EMB ** -0.5)
    d["hy_filt_b1"] = nrm((DEPTH, FILTER_HIDDEN), 0.1)
    d["hy_filt_freq1"] = gain(FILTER_HIDDEN)
    d["hy_filt_w2"] = nrm((DEPTH, FILTER_HIDDEN, FILTER_HIDDEN), FILTER_HIDDEN ** -0.5)
    d["hy_filt_b2"] = nrm((DEPTH, FILTER_HIDDEN), 0.1)
    d["hy_filt_freq2"] = gain(FILTER_HIDDEN)
    d["hy_filt_w3"] = nrm((DEPTH, FILTER_HIDDEN, 2 * D_HYENA), FILTER_HIDDEN ** -0.5)
    d["hy_bias"] = nrm((DEPTH, D_HYENA), 0.5)
    d["hy_w_out"] = nrm((DEPTH, D_HYENA, D_MODEL), D_HYENA ** -0.5)
    d["cv_dw_w"] = nrm((DEPTH, CONF_KERNEL, D_CONV), CONF_KERNEL ** -0.5)
    d["cv_dw_b"] = nrm((DEPTH, D_CONV), 0.02)
    d["cv_ln_g"] = gain(D_CONV)
    d["cv_ln_b"] = nrm((DEPTH, D_CONV), 0.02)
    d["cv_w_out"] = nrm((DEPTH, D_CONV, D_MODEL), D_CONV ** -0.5)
    d["w_out"] = nrm((DEPTH, D_MODEL, D_MODEL), D_MODEL ** -0.5)
    d["mix_norm_post"] = gain(D_MODEL)
    d["ffn2_norm_pre"] = gain(D_MODEL)
    d["ffn2_w_gate"] = nrm((DEPTH, D_MODEL, D_FF), D_MODEL ** -0.5)
    d["ffn2_w_up"] = nrm((DEPTH, D_MODEL, D_FF), D_MODEL ** -0.5)
    d["ffn2_w_down"] = nrm((DEPTH, D_FF, D_MODEL), D_FF ** -0.5)
    d["ffn2_norm_post"] = gain(D_MODEL)
    return d


def reference(x_prompt, x_sample, ffn1_norm_pre, ffn1_w_gate, ffn1_w_up, ffn1_w_down, ffn1_norm_post,
              mix_norm_pre, w_in, hy_short_w, hy_short_b,
              hy_filt_w1, hy_filt_b1, hy_filt_freq1, hy_filt_w2, hy_filt_b2, hy_filt_freq2, hy_filt_w3,
              hy_bias, hy_w_out, cv_dw_w, cv_dw_b, cv_ln_g, cv_ln_b, cv_w_out, w_out, mix_norm_post,
              ffn2_norm_pre, ffn2_w_gate, ffn2_w_up, ffn2_w_down, ffn2_norm_post):
    y_prompt = x_prompt
    y_sample = x_sample
    for l in range(DEPTH):
        params = (ffn1_norm_pre[l], ffn1_w_gate[l], ffn1_w_up[l], ffn1_w_down[l], ffn1_norm_post[l],
                  mix_norm_pre[l], w_in[l], hy_short_w[l], hy_short_b[l],
                  hy_filt_w1[l], hy_filt_b1[l], hy_filt_freq1[l], hy_filt_w2[l], hy_filt_b2[l],
                  hy_filt_freq2[l], hy_filt_w3[l],
                  hy_bias[l], hy_w_out[l], cv_dw_w[l], cv_dw_b[l], cv_ln_g[l], cv_ln_b[l], cv_w_out[l],
                  w_out[l], mix_norm_post[l],
                  ffn2_norm_pre[l], ffn2_w_gate[l], ffn2_w_up[l], ffn2_w_down[l], ffn2_norm_post[l])
        y_prompt = encoder_layer(y_prompt, *params)
        y_sample = encoder_layer(y_sample, *params)
    return (y_prompt, y_sample)
```

```python
import functools
import math

import jax
import jax.numpy as jnp
import numpy as np
from jax import lax
from jax.experimental import pallas as pl
from jax.experimental.pallas import tpu as pltpu

D_MODEL = 1024
D_HYENA = 512
D_CONV = 512
D_FF = 2816
HYENA_SHORT = 3
CONF_KERNEL = 31
FILTER_EMB = 33
FILTER_BANDS = (FILTER_EMB - 1) // 2
FILTER_HIDDEN = 64
DECAY_TARGET = 1e-2
FAST_DECAY_PCT = 0.3
SLOW_DECAY_PCT = 1.5
EPS = 1e-6

F32 = jnp.float32
BF16 = jnp.bfloat16

LANES = 128
SUBLANES = 8
VMEM_LIMIT_BYTES = 60 * 1024 * 1024

SEQ = 8192
NFFT = 2 * SEQ
N2 = 128
N1 = NFFT // N2
T1 = SEQ // N2
F1P = 72
CHUNK = LANES
N_CHUNKS = D_HYENA // CHUNK
F1_GROUP = 8
N_GROUPS = F1P // F1_GROUP
GROUP_LANES = F1_GROUP * CHUNK
PITCH_W = N2 + SUBLANES
PITCH_A = 2 * F1P + SUBLANES
PITCH_B = 2 * N2 + SUBLANES
S_ROWS = max(N2 * PITCH_A, F1P * PITCH_B)

TOKEN_TILE = 256
CONV_HALO = 16
SHORT_HALO = 8


@functools.lru_cache(maxsize=None)
def _fft_tables():
    t1 = np.arange(T1, dtype=np.int64)
    t2 = np.arange(N2, dtype=np.int64)
    f1 = np.arange(F1P, dtype=np.int64)
    keep = (f1 <= N1 // 2).astype(np.float64)
    k = (f1[None, :, None] * (N2 * t1[None, None, :] + t2[:, None, None])) % NFFT
    ph = 2.0 * np.pi * k.astype(np.float64) / NFFT
    fwd = np.concatenate([np.cos(ph) * keep[None, :, None],
                          -np.sin(ph) * keep[None, :, None]], axis=1)
    wgt = np.where((f1 == 0) | (f1 == N1 // 2), 1.0, 2.0) * keep / NFFT
    phq = np.transpose(ph, (0, 2, 1))
    inv = np.concatenate([np.cos(phq) * wgt[None, None, :],
                          -np.sin(phq) * wgt[None, None, :]], axis=2)
    a = np.arange(N2, dtype=np.int64)
    th = 2.0 * np.pi * ((a[:, None] * a[None, :]) % N2).astype(np.float64) / N2
    c, s = np.cos(th), np.sin(th)
    f2_fwd = np.block([[c, s], [-s, c]])
    f2_inv = np.block([[c, -s], [s, c]])
    f32 = np.float32
    return fwd.astype(f32), inv.astype(f32), f2_fwd.astype(f32), f2_inv.astype(f32)


def _filter_constants():
    bands = np.linspace(1e-4, FILTER_BANDS - 1, FILTER_BANDS).astype(np.float32)
    max_decay = math.log(DECAY_TARGET) / FAST_DECAY_PCT
    min_decay = math.log(DECAY_TARGET) / SLOW_DECAY_PCT
    deltas = np.abs(np.linspace(min_decay, max_decay, D_HYENA)).astype(np.float32)
    return bands.reshape(1, FILTER_BANDS), deltas.reshape(1, D_HYENA)


def _whole(shape):
    zeros = (0,) * len(shape)
    return pl.BlockSpec(shape, lambda *_: zeros, pipeline_mode=pl.Buffered(1))


def _rms(x, g):
    return x * lax.rsqrt(jnp.mean(x * x, axis=-1, keepdims=True) + EPS) * g


def _dot(a, b):
    return jnp.dot(a, b, preferred_element_type=F32)


def _swiglu_half_step(x, g_pre, wg_ref, wu_ref, wd_ref, g_post):
    h = _rms(x, g_pre).astype(BF16)
    gate = _dot(h, wg_ref[...])
    up = _dot(h, wu_ref[...])
    act = (gate * jax.nn.sigmoid(gate) * up).astype(BF16)
    return x + 0.5 * _rms(_dot(act, wd_ref[...]), g_post)


def _t1_dft(w_ref, s_ref, fwd_ref):
    def body(t2, carry):
        sig = w_ref[pl.ds(t2, T1, stride=PITCH_W), :]
        a = _dot(fwd_ref[t2], sig.astype(BF16))
        s_ref[pl.ds(pl.multiple_of(t2 * PITCH_A, SUBLANES), 2 * F1P), :] = a
        return carry
    lax.fori_loop(0, N2, body, 0, unroll=4)


def _gather_t2_major(s_ref, r_ref):
    def body(g, carry):
        for k in range(F1_GROUP):
            f1 = g * F1_GROUP + k
            re = s_ref[pl.ds(f1, N2, stride=PITCH_A), :]
            im = s_ref[pl.ds(F1P + f1, N2, stride=PITCH_A), :]
            r_ref[g, 0:N2, k * CHUNK:(k + 1) * CHUNK] = re.astype(BF16)
            r_ref[g, N2:2 * N2, k * CHUNK:(k + 1) * CHUNK] = im.astype(BF16)
        return carry
    lax.fori_loop(0, N_GROUPS, body, 0)


def _forward_spectrum_rows(w_ref, s_ref, r_ref, fwd_ref):
    _t1_dft(w_ref, s_ref, fwd_ref)
    _gather_t2_major(s_ref, r_ref)


def _filter_kernel(w1t_ref, w1c_ref, w1s_ref, b1_ref, fr1_ref, w2_ref, b2_ref, fr2_ref,
                   w3f_ref, w3b_ref, bands_ref, delta_ref, fwd_ref, f2f_ref,
                   kr_ref, ki_ref, w_ref, s_ref, rf_ref, rb_ref):
    hi = lax.Precision.HIGHEST

    def dot_hi(a, b):
        return jnp.dot(a, b, preferred_element_type=F32, precision=hi)

    def signal_rows(t1, w3_ref):
        n = (t1 * N2 + lax.broadcasted_iota(jnp.int32, (N2, 1), 0)).astype(F32)
        t = n * (1.0 / (SEQ - 1))
        ang = (n * (2.0 * math.pi / SEQ)) * bands_ref[...]
        zw = t * w1t_ref[...] + dot_hi(jnp.cos(ang), w1c_ref[...]) - dot_hi(jnp.sin(ang), w1s_ref[...])
        h = jnp.sin(fr1_ref[...] * (zw + b1_ref[...]))
        h = jnp.sin(fr2_ref[...] * (dot_hi(h, w2_ref[...]) + b2_ref[...]))
        return dot_hi(h, w3_ref[...]) * jnp.exp(-t * delta_ref[...])

    def fill(w3_ref):
        def body(t1, ss):
            rows = signal_rows(t1, w3_ref)
            w_ref[pl.ds(pl.multiple_of(t1 * PITCH_W, SUBLANES), N2), :] = rows
            return ss + jnp.sum(rows * rows, axis=0, keepdims=True)
        return lax.fori_loop(0, T1, body, jnp.zeros((1, CHUNK), F32))

    ss_f = fill(w3f_ref)
    f0 = w_ref[0:1, :]
    _forward_spectrum_rows(w_ref, s_ref, rf_ref, fwd_ref)
    ss_b = fill(w3b_ref)
    b0 = w_ref[0:1, :]
    _forward_spectrum_rows(w_ref, s_ref, rb_ref, fwd_ref)
    scale = lax.rsqrt(ss_f + ss_b + 2.0 * f0 * b0 + EPS)
    scale = jnp.concatenate([scale] * F1_GROUP, axis=1)

    def body(g, carry):
        xf = _dot(f2f_ref[...], rf_ref[g])
        xb = _dot(f2f_ref[...], rb_ref[g])
        kr_ref[0, g] = scale * (xf[0:N2] + xb[0:N2])
        ki_ref[0, g] = scale * (xf[N2:2 * N2] - xb[N2:2 * N2])
        return carry
    lax.fori_loop(0, N_GROUPS, body, 0)


def _filter_spectrum(w1, b1, fr1, w2, b2, fr2, w3):
    fwd, _, f2f, _ = _fft_tables()
    bands, deltas = _filter_constants()
    row = lambda v: v.reshape(1, -1)
    small = lambda shape: pl.BlockSpec(shape, lambda c: (0,) * len(shape))
    spec_out = pl.BlockSpec((1, N_GROUPS, N2, GROUP_LANES), lambda c: (c, 0, 0, 0))
    out_shape = jax.ShapeDtypeStruct((N_CHUNKS, N_GROUPS, N2, GROUP_LANES), F32)
    return pl.pallas_call(
        _filter_kernel,
        grid=(N_CHUNKS,),
        in_specs=[
            small((1, FILTER_HIDDEN)), small((FILTER_BANDS, FILTER_HIDDEN)),
            small((FILTER_BANDS, FILTER_HIDDEN)), small((1, FILTER_HIDDEN)),
            small((1, FILTER_HIDDEN)), small((FILTER_HIDDEN, FILTER_HIDDEN)),
            small((1, FILTER_HIDDEN)), small((1, FILTER_HIDDEN)),
            pl.BlockSpec((FILTER_HIDDEN, CHUNK), lambda c: (0, c)),
            pl.BlockSpec((FILTER_HIDDEN, CHUNK), lambda c: (0, N_CHUNKS + c)),
            small((1, FILTER_BANDS)),
            pl.BlockSpec((1, CHUNK), lambda c: (0, c)),
            _whole((N2, 2 * F1P, T1)), _whole((2 * N2, 2 * N2)),
        ],
        out_specs=[spec_out, spec_out],
        out_shape=[out_shape, out_shape],
        scratch_shapes=[
            pltpu.VMEM((T1 * PITCH_W, CHUNK), F32),
            pltpu.VMEM((S_ROWS, CHUNK), F32),
            pltpu.VMEM((N_GROUPS, 2 * N2, GROUP_LANES), BF16),
            pltpu.VMEM((N_GROUPS, 2 * N2, GROUP_LANES), BF16),
        ],
        compiler_params=pltpu.CompilerParams(
            dimension_semantics=("arbitrary",), vmem_limit_bytes=VMEM_LIMIT_BYTES),
        name="filter_spectrum",
    )(w1[0:1], w1[1:1 + FILTER_BANDS], w1[1 + FILTER_BANDS:], row(b1), row(fr1), w2, row(b2),
      row(fr2), w3, w3, jnp.asarray(bands), jnp.asarray(deltas),
      jnp.asarray(fwd).astype(BF16), jnp.asarray(f2f).astype(BF16))


def _ffn_proj_kernel(x_ref, g_pre_ref, wg_ref, wu_ref, wd_ref, g_post_ref, g_mix_ref, win_ref,
                     x1_ref, hy_ref, glu_ref):
    x1 = _swiglu_half_step(x_ref[...], g_pre_ref[...], wg_ref, wu_ref, wd_ref, g_post_ref[...])
    x1_ref[...] = x1
    u = _rms(x1, g_mix_ref[...]).astype(BF16)
    proj = _dot(u, win_ref[...])
    o1 = 3 * D_HYENA
    hy_ref[...] = proj[:, :o1]
    glu_ref[...] = proj[:, o1:o1 + D_CONV] * jax.nn.sigmoid(proj[:, o1 + D_CONV:])


def _ffn_proj(x, g_pre, wg, wu, wd, g_post, g_mix, win_a):
    tokens = x.shape[0]
    tile = lambda width: pl.BlockSpec((TOKEN_TILE, width), lambda i: (i, 0))
    n_a = win_a.shape[1]
    return pl.pallas_call(
        _ffn_proj_kernel,
        grid=(tokens // TOKEN_TILE,),
        in_specs=[tile(D_MODEL), _whole((1, D_MODEL)), _whole((D_MODEL, D_FF)),
                  _whole((D_MODEL, D_FF)), _whole((D_FF, D_MODEL)), _whole((1, D_MODEL)),
                  _whole((1, D_MODEL)), _whole((D_MODEL, n_a))],
        out_specs=[tile(D_MODEL), tile(3 * D_HYENA), tile(D_CONV)],
        out_shape=[jax.ShapeDtypeStruct((tokens, D_MODEL), F32),
                   jax.ShapeDtypeStruct((tokens, 3 * D_HYENA), F32),
                   jax.ShapeDtypeStruct((tokens, D_CONV), F32)],
        compiler_params=pltpu.CompilerParams(
            dimension_semantics=("arbitrary",), vmem_limit_bytes=VMEM_LIMIT_BYTES),
        name="ffn_proj",
    )(x, g_pre, wg, wu, wd, g_post, g_mix, win_a)


def _long_conv_kernel(x1p_ref, vp_ref, swx_ref, sbx_ref, swv_ref, sbv_ref, hb_ref,
                      fwd_ref, inv_ref, f2f_ref, f2i_ref, kr_ref, ki_ref,
                      z_ref, w_ref, s_ref, r_ref):
    zeros_halo = jnp.zeros((SUBLANES, CHUNK), F32)

    def short_conv(ext, sw_ref, sb_ref):
        lo = SUBLANES
        return (sw_ref[0:1, :] * ext[lo - 1:lo - 1 + N2] + sw_ref[1:2, :] * ext[lo:lo + N2]
                + sw_ref[2:3, :] * ext[lo + 1:lo + 1 + N2] + sb_ref[...])

    def gated_rows(ext_x, ext_v):
        return short_conv(ext_v, swv_ref, sbv_ref) * short_conv(ext_x, swx_ref, sbx_ref)

    def edge_ext(ref, first):
        if first:
            return jnp.concatenate([zeros_halo, ref[0, 0:N2 + SUBLANES, :]], axis=0)
        return jnp.concatenate([ref[0, SEQ - N2 - SUBLANES:SEQ, :], zeros_halo], axis=0)

    w_ref[0:N2, :] = gated_rows(edge_ext(x1p_ref, True), edge_ext(vp_ref, True))
    last = (T1 - 1) * PITCH_W
    w_ref[last:last + N2, :] = gated_rows(edge_ext(x1p_ref, False), edge_ext(vp_ref, False))

    def interior(t1, carry):
        start = pl.multiple_of(t1 * N2 - SUBLANES, SUBLANES)
        ext_x = x1p_ref[0, pl.ds(start, N2 + 2 * SUBLANES), :]
        ext_v = vp_ref[0, pl.ds(start, N2 + 2 * SUBLANES), :]
        w_ref[pl.ds(pl.multiple_of(t1 * PITCH_W, SUBLANES), N2), :] = gated_rows(ext_x, ext_v)
        return carry
    lax.fori_loop(1, T1 - 1, interior, 0)

    _forward_spectrum_rows(w_ref, s_ref, r_ref, fwd_ref)

    def freq_body(g, carry):
        x = _dot(f2f_ref[...], r_ref[g])
        xr, xi = x[0:N2], x[N2:2 * N2]
        kr, ki = kr_ref[0, g], ki_ref[0, g]
        y = jnp.concatenate([xr * kr - xi * ki, xr * ki + xi * kr], axis=0).astype(BF16)
        b = _dot(f2i_ref[...], y)
        for k in range(F1_GROUP):
            row0 = pl.multiple_of((g * F1_GROUP + k) * PITCH_B, SUBLANES)
            s_ref[pl.ds(row0, 2 * N2), :] = b[:, k * CHUNK:(k + 1) * CHUNK]
        return carry
    lax.fori_loop(0, N_GROUPS, freq_body, 0)

    def inv_body(t2, carry):
        g = jnp.concatenate([s_ref[pl.ds(t2, F1P, stride=PITCH_B), :],
                             s_ref[pl.ds(N2 + t2, F1P, stride=PITCH_B), :]], axis=0)
        y = _dot(inv_ref[t2], g.astype(BF16))
        sig = w_ref[pl.ds(t2, T1, stride=PITCH_W), :]
        w_ref[pl.ds(t2, T1, stride=PITCH_W), :] = y + sig * hb_ref[...]
        return carry
    lax.fori_loop(0, N2, inv_body, 0, unroll=4)

    def out_body(t1, carry):
        z_ref[0, pl.ds(pl.multiple_of(t1 * N2, SUBLANES), N2), :] = (
            w_ref[pl.ds(pl.multiple_of(t1 * PITCH_W, SUBLANES), N2), :])
        return carry
    lax.fori_loop(0, T1, out_body, 0)


def _long_conv(hy, short_w, short_b, hy_bias, kr, ki):
    batch = hy.shape[0]
    fwd, inv, f2f, f2i = _fft_tables()
    col = lambda base: pl.BlockSpec((1, SEQ, CHUNK), lambda c, b: (b, 0, base + c),
                                    pipeline_mode=pl.Buffered(1))
    par = lambda rows, base: pl.BlockSpec((rows, CHUNK), lambda c, b: (0, base + c))
    spec_k = pl.BlockSpec((1, N_GROUPS, N2, GROUP_LANES), lambda c, b: (c, 0, 0, 0),
                          pipeline_mode=pl.Buffered(1))
    sb = short_b.reshape(1, -1)
    return pl.pallas_call(
        _long_conv_kernel,
        grid=(N_CHUNKS, batch),
        in_specs=[col(N_CHUNKS), col(2 * N_CHUNKS),
                  par(HYENA_SHORT, N_CHUNKS), par(1, N_CHUNKS),
                  par(HYENA_SHORT, 2 * N_CHUNKS), par(1, 2 * N_CHUNKS), par(1, 0),
                  _whole((N2, 2 * F1P, T1)), _whole((N2, T1, 2 * F1P)),
                  _whole((2 * N2, 2 * N2)), _whole((2 * N2, 2 * N2)), spec_k, spec_k],
        out_specs=pl.BlockSpec((1, SEQ, CHUNK), lambda c, b: (b, 0, c)),
        out_shape=jax.ShapeDtypeStruct((batch, SEQ, D_HYENA), F32),
        scratch_shapes=[
            pltpu.VMEM((T1 * PITCH_W, CHUNK), F32),
            pltpu.VMEM((S_ROWS, CHUNK), F32),
            pltpu.VMEM((N_GROUPS, 2 * N2, GROUP_LANES), BF16),
        ],
        compiler_params=pltpu.CompilerParams(
            dimension_semantics=("arbitrary", "arbitrary"), vmem_limit_bytes=VMEM_LIMIT_BYTES),
        name="long_conv",
    )(hy, hy, short_w, sb, short_w, sb, hy_bias.reshape(1, -1),
      jnp.asarray(fwd).astype(BF16), jnp.asarray(inv).astype(BF16),
      jnp.asarray(f2f).astype(BF16), jnp.asarray(f2i).astype(BF16), kr, ki)


def _mix_ffn_kernel(x1_ref, z_ref, x0_ref, x0_prev_ref, x0_next_ref, glu_ref, glu_prev_ref,
                    glu_next_ref, g_mix_ref, wgate_ref, sw_ref, sb_ref, hyw_ref,
                    dww_ref, dwb_ref, lng_ref, lnb_ref, cvw_ref, wout_ref, g_mixpost_ref,
                    g_pre_ref, wg_ref, wu_ref, wd_ref, g_post_ref,
                    out_ref, x0_ext, glu_ext):
    tm = TOKEN_TILE
    tiles_per_seq = SEQ // tm
    pos = pl.program_id(0) % tiles_per_seq
    first = pos == 0
    last = pos == tiles_per_seq - 1

    h = SHORT_HALO
    x0_ext[0:h, :] = jnp.where(first, 0.0, x0_prev_ref[...])
    x0_ext[h:h + tm, :] = x0_ref[...]
    x0_ext[h + tm:, :] = jnp.where(last, 0.0, x0_next_ref[...])
    x0 = sb_ref[...]
    for j in range(HYENA_SHORT):
        off = h + j - HYENA_SHORT // 2
        x0 = x0 + sw_ref[j:j + 1, :] * x0_ext[off:off + tm, :]
    y_a = _dot((z_ref[...] * x0).astype(BF16), hyw_ref[...])

    h = CONV_HALO
    glu_ext[0:h, :] = jnp.where(first, 0.0, glu_prev_ref[...])
    glu_ext[h:h + tm, :] = glu_ref[...]
    glu_ext[h + tm:, :] = jnp.where(last, 0.0, glu_next_ref[...])
    c = dwb_ref[...]
    for j in range(CONF_KERNEL):
        off = h + j - CONF_KERNEL // 2
        c = c + dww_ref[j:j + 1, :] * glu_ext[off:off + tm, :]
    mu = jnp.mean(c, axis=-1, keepdims=True)
    cc = c - mu
    c = cc * lax.rsqrt(jnp.mean(cc * cc, axis=-1, keepdims=True) + EPS) * lng_ref[...] + lnb_ref[...]
    c = c * jax.nn.sigmoid(c)
    y_b = _dot(c.astype(BF16), cvw_ref[...])

    x1 = x1_ref[...]
    u = _rms(x1, g_mix_ref[...]).astype(BF16)
    gates = _dot(u, wgate_ref[...])
    merged = (jax.nn.sigmoid(gates[:, :D_MODEL]) * y_a + jax.nn.sigmoid(gates[:, D_MODEL:]) * y_b)
    m = _dot(merged.astype(BF16), wout_ref[...])
    x2 = x1 + _rms(m, g_mixpost_ref[...])
    out_ref[...] = _swiglu_half_step(x2, g_pre_ref[...], wg_ref, wu_ref, wd_ref, g_post_ref[...])


def _mix_ffn(x1, z, hy, glu, g_mix, w_gates, short_w, short_b, hy_w_out, dw_w, dw_b, ln_g, ln_b,
             cv_w_out, w_out, g_mixpost, g_pre, wg, wu, wd, g_post):
    tokens = x1.shape[0]
    tm = TOKEN_TILE
    n_short = tokens // SHORT_HALO
    n_conv = tokens // CONV_HALO
    tile = lambda width: pl.BlockSpec((tm, width), lambda i: (i, 0))
    halo_prev = lambda rows, width: pl.BlockSpec(
        (rows, width), lambda i: (jnp.maximum(i * (tm // rows) - 1, 0), 0))
    halo_next = lambda rows, width, n: pl.BlockSpec(
        (rows, width), lambda i: (jnp.minimum((i + 1) * (tm // rows), n - 1), 0))
    return pl.pallas_call(
        _mix_ffn_kernel,
        grid=(tokens // tm,),
        in_specs=[tile(D_MODEL), tile(D_HYENA),
                  tile(D_HYENA), halo_prev(SHORT_HALO, D_HYENA), halo_next(SHORT_HALO, D_HYENA, n_short),
                  tile(D_CONV), halo_prev(CONV_HALO, D_CONV), halo_next(CONV_HALO, D_CONV, n_conv),
                  _whole((1, D_MODEL)), _whole((D_MODEL, 2 * D_MODEL)),
                  pl.BlockSpec((HYENA_SHORT, D_HYENA), lambda i: (0, 0)),
                  pl.BlockSpec((1, D_HYENA), lambda i: (0, 0)),
                  _whole((D_HYENA, D_MODEL)),
                  _whole((CONF_KERNEL, D_CONV)), _whole((1, D_CONV)), _whole((1, D_CONV)),
                  _whole((1, D_CONV)), _whole((D_CONV, D_MODEL)), _whole((D_MODEL, D_MODEL)),
                  _whole((1, D_MODEL)), _whole((1, D_MODEL)), _whole((D_MODEL, D_FF)),
                  _whole((D_MODEL, D_FF)), _whole((D_FF, D_MODEL)), _whole((1, D_MODEL))],
        out_specs=tile(D_MODEL),
        out_shape=jax.ShapeDtypeStruct((tokens, D_MODEL), F32),
        scratch_shapes=[pltpu.VMEM((tm + 2 * SHORT_HALO, D_HYENA), F32),
                        pltpu.VMEM((tm + 2 * CONV_HALO, D_CONV), F32)],
        compiler_params=pltpu.CompilerParams(
            dimension_semantics=("arbitrary",), vmem_limit_bytes=VMEM_LIMIT_BYTES),
        name="mix_ffn",
    )(x1, z, hy, hy, hy, glu, glu, glu, g_mix, w_gates, short_w, short_b, hy_w_out, dw_w, dw_b,
      ln_g, ln_b, cv_w_out, w_out, g_mixpost, g_pre, wg, wu, wd, g_post)


def _encoder_layer(xs, p):
    row = lambda v: v.reshape(1, -1)
    bf = lambda w: w.astype(BF16)
    o1 = 3 * D_HYENA
    o2 = o1 + 2 * D_CONV
    kr, ki = _filter_spectrum(p["hy_filt_w1"], p["hy_filt_b1"], p["hy_filt_freq1"], p["hy_filt_w2"],
                              p["hy_filt_b2"], p["hy_filt_freq2"], p["hy_filt_w3"])
    win_a = bf(p["w_in"][:, :o2])
    w_gates = bf(p["w_in"][:, o2:])
    ffn1 = (row(p["ffn1_norm_pre"]), bf(p["ffn1_w_gate"]), bf(p["ffn1_w_up"]), bf(p["ffn1_w_down"]),
            row(p["ffn1_norm_post"]))
    ffn2 = (row(p["ffn2_norm_pre"]), bf(p["ffn2_w_gate"]), bf(p["ffn2_w_up"]), bf(p["ffn2_w_down"]),
            row(p["ffn2_norm_post"]))
    g_mix = row(p["mix_norm_pre"])
    outs = []
    for x in xs:
        batch, seq, _ = x.shape
        assert seq == SEQ and (batch * seq) % TOKEN_TILE == 0
        x1, hy, glu = _ffn_proj(x.reshape(batch * seq, D_MODEL), *ffn1, g_mix, win_a)
        z = _long_conv(hy.reshape(batch, seq, o1), p["hy_short_w"], p["hy_short_b"], p["hy_bias"], kr, ki)
        y = _mix_ffn(x1, z.reshape(batch * seq, D_HYENA), hy, glu, g_mix, w_gates,
                     p["hy_short_w"][:, :D_HYENA], row(p["hy_short_b"][:D_HYENA]), bf(p["hy_w_out"]),
                     p["cv_dw_w"], row(p["cv_dw_b"]), row(p["cv_ln_g"]), row(p["cv_ln_b"]),
                     bf(p["cv_w_out"]), bf(p["w_out"]), row(p["mix_norm_post"]), *ffn2)
        outs.append(y.reshape(batch, seq, D_MODEL))
    return outs


_PARAM_NAMES = (
    "ffn1_norm_pre", "ffn1_w_gate", "ffn1_w_up", "ffn1_w_down", "ffn1_norm_post",
    "mix_norm_pre", "w_in", "hy_short_w", "hy_short_b",
    "hy_filt_w1", "hy_filt_b1", "hy_filt_freq1", "hy_filt_w2", "hy_filt_b2", "hy_filt_freq2",
    "hy_filt_w3", "hy_bias", "hy_w_out", "cv_dw_w", "cv_dw_b", "cv_ln_g", "cv_ln_b", "cv_w_out",
    "w_out", "mix_norm_post",
    "ffn2_norm_pre", "ffn2_w_gate", "ffn2_w_up", "ffn2_w_down", "ffn2_norm_post")


def kernel(x_prompt, x_sample, ffn1_norm_pre, ffn1_w_gate, ffn1_w_up, ffn1_w_down, ffn1_norm_post, mix_norm_pre, w_in, hy_short_w, hy_short_b, hy_filt_w1, hy_filt_b1, hy_filt_freq1, hy_filt_w2, hy_filt_b2, hy_filt_freq2, hy_filt_w3, hy_bias, hy_w_out, cv_dw_w, cv_dw_b, cv_ln_g, cv_ln_b, cv_w_out, w_out, mix_norm_post, ffn2_norm_pre, ffn2_w_gate, ffn2_w_up, ffn2_w_down, ffn2_norm_post):
    stacked = dict(zip(_PARAM_NAMES, (
        ffn1_norm_pre, ffn1_w_gate, ffn1_w_up, ffn1_w_down, ffn1_norm_post,
        mix_norm_pre, w_in, hy_short_w, hy_short_b,
        hy_filt_w1, hy_filt_b1, hy_filt_freq1, hy_filt_w2, hy_filt_b2, hy_filt_freq2,
        hy_filt_w3, hy_bias, hy_w_out, cv_dw_w, cv_dw_b, cv_ln_g, cv_ln_b, cv_w_out,
        w_out, mix_norm_post,
        ffn2_norm_pre, ffn2_w_gate, ffn2_w_up, ffn2_w_down, ffn2_norm_post)))
    xs = [x_prompt, x_sample]
    for layer in range(ffn1_norm_pre.shape[0]):
        xs = _encoder_layer(xs, {k: v[layer] for k, v in stacked.items()})
    return (xs[0], xs[1])
```

```python
import functools
import math

import jax
import jax.numpy as jnp
import numpy as np
from jax import lax
from jax.experimental import pallas as pl
from jax.experimental.pallas import tpu as pltpu

D_MODEL = 1024
D_HYENA = 512
D_CONV = 512
D_FF = 2816
HYENA_SHORT = 3
CONF_KERNEL = 31
FILTER_EMB = 33
FILTER_BANDS = (FILTER_EMB - 1) // 2
FILTER_HIDDEN = 64
DECAY_TARGET = 1e-2
FAST_DECAY_PCT = 0.3
SLOW_DECAY_PCT = 1.5
EPS = 1e-6

F32 = jnp.float32
BF16 = jnp.bfloat16

LANES = 128
SUBLANES = 8
VMEM_LIMIT_BYTES = 60 * 1024 * 1024

SEQ = 8192
NFFT = 2 * SEQ
N2 = 128
N1 = NFFT // N2
T1 = SEQ // N2
F1P = 72
CHUNK = LANES
N_CHUNKS = D_HYENA // CHUNK
F1_GROUP = 8
N_GROUPS = F1P // F1_GROUP
GROUP_LANES = F1_GROUP * CHUNK
PITCH_A = 2 * F1P + SUBLANES
PITCH_B = 2 * N2 + SUBLANES
PITCH_Y = N2 + SUBLANES
S_ROWS = max(N2 * PITCH_A, F1P * PITCH_B)
T2_UNROLL = 16
FILTER_COLS = 1024

TOKEN_TILE = 256
CONV_HALO = 16
SHORT_HALO = 8
ROW_STEP = 2


@functools.lru_cache(maxsize=None)
def _fft_tables():
    t1 = np.arange(T1, dtype=np.int64)
    t2 = np.arange(N2, dtype=np.int64)
    f1 = np.arange(F1P, dtype=np.int64)
    keep = (f1 <= N1 // 2).astype(np.float64)
    k = (f1[None, :, None] * (N2 * t1[None, None, :] + t2[:, None, None])) % NFFT
    ph = 2.0 * np.pi * k.astype(np.float64) / NFFT
    fwd = np.concatenate([np.cos(ph) * keep[None, :, None],
                          -np.sin(ph) * keep[None, :, None]], axis=1)
    wgt = np.where((f1 == 0) | (f1 == N1 // 2), 1.0, 2.0) * keep / NFFT
    phq = np.transpose(ph, (0, 2, 1))
    inv = np.concatenate([np.cos(phq) * wgt[None, None, :],
                          -np.sin(phq) * wgt[None, None, :]], axis=2)
    a = np.arange(N2, dtype=np.int64)
    th = 2.0 * np.pi * ((a[:, None] * a[None, :]) % N2).astype(np.float64) / N2
    c, s = np.cos(th), np.sin(th)
    f2_fwd = np.block([[c, s], [-s, c]])
    f2_inv = np.block([[c, -s], [s, c]])
    f32 = np.float32
    return fwd.astype(f32), inv.astype(f32), f2_fwd.astype(f32), f2_inv.astype(f32)


def _filter_constants():
    bands = np.linspace(1e-4, FILTER_BANDS - 1, FILTER_BANDS).astype(np.float32)
    max_decay = math.log(DECAY_TARGET) / FAST_DECAY_PCT
    min_decay = math.log(DECAY_TARGET) / SLOW_DECAY_PCT
    deltas = np.abs(np.linspace(min_decay, max_decay, D_HYENA)).astype(np.float32)
    return bands.reshape(FILTER_BANDS, 1), deltas.reshape(1, D_HYENA)


def _whole(shape):
    zeros = (0,) * len(shape)
    return pl.BlockSpec(shape, lambda *_: zeros, pipeline_mode=pl.Buffered(1))


def _rms(x, g):
    return x * lax.rsqrt(jnp.mean(x * x, axis=-1, keepdims=True) + EPS) * g


def _dot(a, b):
    return jnp.dot(a, b, preferred_element_type=F32)


def _swiglu_half_step(x, g_pre, wg_ref, wu_ref, wd_ref, g_post):
    h = _rms(x, g_pre).astype(BF16)
    gate = _dot(h, wg_ref[...])
    up = _dot(h, wu_ref[...])
    act = (gate * jax.nn.sigmoid(gate) * up).astype(BF16)
    return x + 0.5 * _rms(_dot(act, wd_ref[...]), g_post)


def _t1_dft(load_signal, s_ref, fwd_ref):
    def body(t2, carry):
        a = _dot(fwd_ref[t2], load_signal(t2).astype(BF16))
        s_ref[pl.ds(pl.multiple_of(t2 * PITCH_A, SUBLANES), 2 * F1P), :] = a
        return carry
    lax.fori_loop(0, N2, body, 0, unroll=T2_UNROLL)


def _gather_t2_major(s_ref, r_ref):
    def body(g, carry):
        for k in range(F1_GROUP):
            f1 = g * F1_GROUP + k
            re = s_ref[pl.ds(f1, N2, stride=PITCH_A), :]
            im = s_ref[pl.ds(F1P + f1, N2, stride=PITCH_A), :]
            r_ref[g, 0:N2, k * CHUNK:(k + 1) * CHUNK] = re.astype(BF16)
            r_ref[g, N2:2 * N2, k * CHUNK:(k + 1) * CHUNK] = im.astype(BF16)
        return carry
    lax.fori_loop(0, N_GROUPS, body, 0)


def _forward_spectrum_rows(load_signal, s_ref, r_ref, fwd_ref):
    _t1_dft(load_signal, s_ref, fwd_ref)
    _gather_t2_major(s_ref, r_ref)


def _time_major_rows(ref, *lead):
    return lambda t2: ref[(*lead, pl.ds(t2, T1, stride=N2), slice(None))]


def _filter_kernel(w1t_ref, w1c_ref, w1s_ref, b1_ref, fr1_ref, w2_ref, b2_ref, fr2_ref,
                   w3f_ref, w3b_ref, bands_ref, delta_ref, fwd_ref, f2f_ref,
                   kr_ref, ki_ref, h_ref, w_ref, s_ref, rf_ref, rb_ref):
    hi = lax.Precision.HIGHEST

    def dot_hi(a, b):
        return jnp.dot(a, b, preferred_element_type=F32, precision=hi)

    @pl.when(pl.program_id(0) == 0)
    def _():
        def body(i, carry):
            col0 = pl.multiple_of(i * FILTER_COLS, FILTER_COLS)
            n = (col0 + lax.broadcasted_iota(jnp.int32, (1, FILTER_COLS), 1)).astype(F32)
            t = n * (1.0 / (SEQ - 1))
            ang = bands_ref[...] * (n * (2.0 * math.pi / SEQ))
            zw = (w1t_ref[...] * t + dot_hi(w1c_ref[...], jnp.cos(ang))
                  - dot_hi(w1s_ref[...], jnp.sin(ang)))
            h = jnp.sin(fr1_ref[...] * (zw + b1_ref[...]))
            h = jnp.sin(fr2_ref[...] * (dot_hi(w2_ref[...], h) + b2_ref[...]))
            for k in range(FILTER_COLS // N2):
                h_ref[i * (FILTER_COLS // N2) + k] = h[:, k * N2:(k + 1) * N2]
            return carry
        lax.fori_loop(0, SEQ // FILTER_COLS, body, 0)

    def fill(w3_ref):
        def body(t1, ss):
            row0 = pl.multiple_of(t1 * N2, N2)
            n = (row0 + lax.broadcasted_iota(jnp.int32, (N2, 1), 0)).astype(F32)
            decay = jnp.exp(-(n * (1.0 / (SEQ - 1))) * delta_ref[...])
            rows = dot_hi(h_ref[t1].T, w3_ref[...]) * decay
            w_ref[pl.ds(row0, N2), :] = rows
            return ss + jnp.sum(rows * rows, axis=0, keepdims=True)
        return lax.fori_loop(0, T1, body, jnp.zeros((1, CHUNK), F32), unroll=4)

    ss_f = fill(w3f_ref)
    f0 = w_ref[0:1, :]
    _forward_spectrum_rows(_time_major_rows(w_ref), s_ref, rf_ref, fwd_ref)
    ss_b = fill(w3b_ref)
    b0 = w_ref[0:1, :]
    _forward_spectrum_rows(_time_major_rows(w_ref), s_ref, rb_ref, fwd_ref)
    scale = lax.rsqrt(ss_f + ss_b + 2.0 * f0 * b0 + EPS)
    scale = jnp.concatenate([scale] * F1_GROUP, axis=1)

    def body(g, carry):
        xf = _dot(f2f_ref[...], rf_ref[g])
        xb = _dot(f2f_ref[...], rb_ref[g])
        kr_ref[0, g] = scale * (xf[0:N2] + xb[0:N2])
        ki_ref[0, g] = scale * (xf[N2:2 * N2] - xb[N2:2 * N2])
        return carry
    lax.fori_loop(0, N_GROUPS, body, 0)


def _filter_spectrum(w1, b1, fr1, w2, b2, fr2, w3):
    fwd, _, f2f, _ = _fft_tables()
    bands, deltas = _filter_constants()
    col = lambda v: v.reshape(-1, 1)
    small = lambda shape: pl.BlockSpec(shape, lambda c: (0,) * len(shape))
    spec_out = pl.BlockSpec((1, N_GROUPS, N2, GROUP_LANES), lambda c: (c, 0, 0, 0))
    out_shape = jax.ShapeDtypeStruct((N_CHUNKS, N_GROUPS, N2, GROUP_LANES), F32)
    hid = FILTER_HIDDEN
    return pl.pallas_call(
        _filter_kernel,
        grid=(N_CHUNKS,),
        in_specs=[
            small((hid, 1)), small((hid, FILTER_BANDS)), small((hid, FILTER_BANDS)),
            small((hid, 1)), small((hid, 1)), small((hid, hid)), small((hid, 1)), small((hid, 1)),
            pl.BlockSpec((hid, CHUNK), lambda c: (0, c)),
            pl.BlockSpec((hid, CHUNK), lambda c: (0, N_CHUNKS + c)),
            small((FILTER_BANDS, 1)),
            pl.BlockSpec((1, CHUNK), lambda c: (0, c)),
            _whole((N2, 2 * F1P, T1)), _whole((2 * N2, 2 * N2)),
        ],
        out_specs=[spec_out, spec_out],
        out_shape=[out_shape, out_shape],
        scratch_shapes=[
            pltpu.VMEM((T1, hid, N2), F32),
            pltpu.VMEM((SEQ, CHUNK), F32),
            pltpu.VMEM((S_ROWS, CHUNK), F32),
            pltpu.VMEM((N_GROUPS, 2 * N2, GROUP_LANES), BF16),
            pltpu.VMEM((N_GROUPS, 2 * N2, GROUP_LANES), BF16),
        ],
        compiler_params=pltpu.CompilerParams(
            dimension_semantics=("arbitrary",), vmem_limit_bytes=VMEM_LIMIT_BYTES),
        name="filter_spectrum",
    )(col(w1[0]), w1[1:1 + FILTER_BANDS].T, w1[1 + FILTER_BANDS:].T, col(b1), col(fr1), w2.T,
      col(b2), col(fr2), w3, w3, jnp.asarray(bands), jnp.asarray(deltas),
      jnp.asarray(fwd).astype(BF16), jnp.asarray(f2f).astype(BF16))


def _ffn_proj_kernel(x_ref, g_pre_ref, wg_ref, wu_ref, wd_ref, g_post_ref, g_mix_ref, win_ref,
                     sw_ref, sb_ref, dww_ref, dwb_ref, lng_ref, lnb_ref,
                     x1_ref, sig_ref, x0_ref, cact_ref, hy_ext, glu_ext, hy_new, glu_new):
    tm = TOKEN_TILE
    hs, hc = SHORT_HALO, CONV_HALO
    step = pl.program_id(0)
    tiles_per_seq = SEQ // tm
    new_starts_seq = (step + tiles_per_seq - 1) % tiles_per_seq == 0
    cur_starts_seq = step % tiles_per_seq == 0

    def rows(start, size):
        return pl.ds(ROW_STEP * start, size, stride=ROW_STEP)

    def lanes(slab):
        return slice(slab * LANES, (slab + 1) * LANES)

    @pl.when(step == 0)
    def _():
        hy_ext[...] = jnp.zeros_like(hy_ext)
        glu_ext[...] = jnp.zeros_like(glu_ext)
        hy_new[...] = jnp.zeros_like(hy_new)
        glu_new[...] = jnp.zeros_like(glu_new)

    def depthwise(ext, slab, halo, w_ref, b_ref, taps):
        acc = b_ref[:, lanes(slab)]
        for j in range(taps):
            acc = acc + w_ref[j:j + 1, lanes(slab)] * ext[slab, rows(halo + j - taps // 2, tm), :]
        return acc

    n_h = D_HYENA // LANES
    for slab in range(n_h):
        x0_ref[:, lanes(slab)] = depthwise(hy_ext, slab, hs, sw_ref, sb_ref, HYENA_SHORT)
        sig_ref[:, lanes(slab)] = (depthwise(hy_ext, 2 * n_h + slab, hs, sw_ref, sb_ref, HYENA_SHORT)
                                   * depthwise(hy_ext, n_h + slab, hs, sw_ref, sb_ref, HYENA_SHORT))

    c = jnp.concatenate([depthwise(glu_ext, slab, hc, dww_ref, dwb_ref, CONF_KERNEL)
                         for slab in range(D_CONV // LANES)], axis=1)
    mu = jnp.mean(c, axis=-1, keepdims=True)
    cc = c - mu
    c = cc * lax.rsqrt(jnp.mean(cc * cc, axis=-1, keepdims=True) + EPS) * lng_ref[...] + lnb_ref[...]
    cact_ref[...] = (c * jax.nn.sigmoid(c)).astype(BF16)

    x1 = _swiglu_half_step(x_ref[...], g_pre_ref[...], wg_ref, wu_ref, wd_ref, g_post_ref[...])
    x1_ref[...] = x1
    u = _rms(x1, g_mix_ref[...]).astype(BF16)
    proj = _dot(u, win_ref[...])
    o1 = 3 * D_HYENA
    hy = proj[:, :o1]
    glu = proj[:, o1:o1 + D_CONV] * jax.nn.sigmoid(proj[:, o1 + D_CONV:])

    for ext, new, cur, h in ((hy_ext, hy_new, hy, hs), (glu_ext, glu_new, glu, hc)):
        for slab in range(cur.shape[1] // LANES):
            tail = ext[slab, rows(tm, h), :]
            ext[slab, rows(0, h), :] = jnp.where(new_starts_seq, 0.0, tail)
            ext[slab, rows(h, tm), :] = new[:, lanes(slab)]
            ext[slab, rows(h + tm, h), :] = jnp.where(cur_starts_seq, 0.0, cur[0:h, lanes(slab)])
        new[...] = cur


def _ffn_proj(x, g_pre, wg, wu, wd, g_post, g_mix, win_a, short_w, short_b, dw_w, dw_b, ln_g, ln_b):
    tokens = x.shape[0]
    tm = TOKEN_TILE
    n = tokens // tm
    cur = lambda width: pl.BlockSpec((tm, width), lambda s: (jnp.minimum(s, n - 1), 0))
    lag = lambda width: pl.BlockSpec((tm, width), lambda s: (jnp.maximum(s - 2, 0), 0))
    n_a = win_a.shape[1]
    return pl.pallas_call(
        _ffn_proj_kernel,
        grid=(n + 2,),
        in_specs=[cur(D_MODEL), _whole((1, D_MODEL)), _whole((D_MODEL, D_FF)),
                  _whole((D_MODEL, D_FF)), _whole((D_FF, D_MODEL)), _whole((1, D_MODEL)),
                  _whole((1, D_MODEL)), _whole((D_MODEL, n_a)),
                  _whole((HYENA_SHORT, 3 * D_HYENA)), _whole((1, 3 * D_HYENA)),
                  _whole((CONF_KERNEL, D_CONV)), _whole((1, D_CONV)), _whole((1, D_CONV)),
                  _whole((1, D_CONV))],
        out_specs=[cur(D_MODEL), lag(D_HYENA), lag(D_HYENA), lag(D_CONV)],
        out_shape=[jax.ShapeDtypeStruct((tokens, D_MODEL), F32),
                   jax.ShapeDtypeStruct((tokens, D_HYENA), F32),
                   jax.ShapeDtypeStruct((tokens, D_HYENA), F32),
                   jax.ShapeDtypeStruct((tokens, D_CONV), BF16)],
        scratch_shapes=[pltpu.VMEM((3 * D_HYENA // LANES, ROW_STEP * (tm + 2 * SHORT_HALO), LANES), F32),
                        pltpu.VMEM((D_CONV // LANES, ROW_STEP * (tm + 2 * CONV_HALO), LANES), F32),
                        pltpu.VMEM((tm, 3 * D_HYENA), F32),
                        pltpu.VMEM((tm, D_CONV), F32)],
        compiler_params=pltpu.CompilerParams(
            dimension_semantics=("arbitrary",), vmem_limit_bytes=VMEM_LIMIT_BYTES),
        name="ffn_proj",
    )(x, g_pre, wg, wu, wd, g_post, g_mix, win_a, short_w, short_b, dw_w, dw_b, ln_g, ln_b)


def _long_conv_kernel(sig_ref, hb_ref, fwd_ref, inv_ref, f2f_ref, f2i_ref, kr_ref, ki_ref,
                      z_ref, s_ref, r_ref, y_ref):
    sig_rows = _time_major_rows(sig_ref, 0)
    _forward_spectrum_rows(sig_rows, s_ref, r_ref, fwd_ref)

    def freq_body(g, carry):
        x = _dot(f2f_ref[...], r_ref[g])
        xr, xi = x[0:N2], x[N2:2 * N2]
        kr, ki = kr_ref[0, g], ki_ref[0, g]
        y = jnp.concatenate([xr * kr - xi * ki, xr * ki + xi * kr], axis=0).astype(BF16)
        b = _dot(f2i_ref[...], y)
        for k in range(F1_GROUP):
            row0 = pl.multiple_of((g * F1_GROUP + k) * PITCH_B, SUBLANES)
            s_ref[pl.ds(row0, 2 * N2), :] = b[:, k * CHUNK:(k + 1) * CHUNK]
        return carry
    lax.fori_loop(0, N_GROUPS, freq_body, 0, unroll=3)

    def inv_body(t2, carry):
        g = jnp.concatenate([s_ref[pl.ds(t2, F1P, stride=PITCH_B), :],
                             s_ref[pl.ds(N2 + t2, F1P, stride=PITCH_B), :]], axis=0)
        y_ref[pl.ds(t2, T1, stride=PITCH_Y), :] = _dot(inv_ref[t2], g.astype(BF16))
        return carry
    lax.fori_loop(0, N2, inv_body, 0, unroll=T2_UNROLL)

    def out_body(t1, carry):
        rows = pl.ds(pl.multiple_of(t1 * N2, N2), N2)
        y = y_ref[pl.ds(pl.multiple_of(t1 * PITCH_Y, SUBLANES), N2), :]
        z_ref[0, rows, :] = y + sig_ref[0, rows, :] * hb_ref[...]
        return carry
    lax.fori_loop(0, T1, out_body, 0, unroll=4)


def _long_conv(sig, hy_bias, kr, ki):
    batch = sig.shape[0]
    fwd, inv, f2f, f2i = _fft_tables()
    spec_k = pl.BlockSpec((1, N_GROUPS, N2, GROUP_LANES), lambda c, b: (c, 0, 0, 0),
                          pipeline_mode=pl.Buffered(1))
    spec_io = pl.BlockSpec((1, SEQ, CHUNK), lambda c, b: (b, 0, c))
    return pl.pallas_call(
        _long_conv_kernel,
        grid=(N_CHUNKS, batch),
        in_specs=[spec_io, pl.BlockSpec((1, CHUNK), lambda c, b: (0, c)),
                  _whole((N2, 2 * F1P, T1)), _whole((N2, T1, 2 * F1P)),
                  _whole((2 * N2, 2 * N2)), _whole((2 * N2, 2 * N2)), spec_k, spec_k],
        out_specs=spec_io,
        out_shape=jax.ShapeDtypeStruct((batch, SEQ, D_HYENA), F32),
        scratch_shapes=[
            pltpu.VMEM((S_ROWS, CHUNK), F32),
            pltpu.VMEM((N_GROUPS, 2 * N2, GROUP_LANES), BF16),
            pltpu.VMEM((T1 * PITCH_Y, CHUNK), F32),
        ],
        compiler_params=pltpu.CompilerParams(
            dimension_semantics=("arbitrary", "arbitrary"), vmem_limit_bytes=VMEM_LIMIT_BYTES),
        name="long_conv",
    )(sig, hy_bias.reshape(1, -1),
      jnp.asarray(fwd).astype(BF16), jnp.asarray(inv).astype(BF16),
      jnp.asarray(f2f).astype(BF16), jnp.asarray(f2i).astype(BF16), kr, ki)


def _mix_ffn_kernel(x1_ref, z_ref, x0_ref, cact_ref, g_mix_ref, wgate_ref, hyw_ref, cvw_ref,
                    wout_ref, g_mixpost_ref, g_pre_ref, wg_ref, wu_ref, wd_ref, g_post_ref,
                    out_ref):
    y_a = _dot((z_ref[...] * x0_ref[...]).astype(BF16), hyw_ref[...])
    y_b = _dot(cact_ref[...], cvw_ref[...])
    x1 = x1_ref[...]
    u = _rms(x1, g_mix_ref[...]).astype(BF16)
    gates = _dot(u, wgate_ref[...])
    merged = (jax.nn.sigmoid(gates[:, :D_MODEL]) * y_a + jax.nn.sigmoid(gates[:, D_MODEL:]) * y_b)
    m = _dot(merged.astype(BF16), wout_ref[...])
    x2 = x1 + _rms(m, g_mixpost_ref[...])
    out_ref[...] = _swiglu_half_step(x2, g_pre_ref[...], wg_ref, wu_ref, wd_ref, g_post_ref[...])


def _mix_ffn(x1, z, x0, cact, g_mix, w_gates, hy_w_out, cv_w_out, w_out, g_mixpost,
             g_pre, wg, wu, wd, g_post):
    tokens = x1.shape[0]
    tm = TOKEN_TILE
    tile = lambda width: pl.BlockSpec((tm, width), lambda i: (i, 0))
    return pl.pallas_call(
        _mix_ffn_kernel,
        grid=(tokens // tm,),
        in_specs=[tile(D_MODEL), tile(D_HYENA), tile(D_HYENA), tile(D_CONV),
                  _whole((1, D_MODEL)), _whole((D_MODEL, 2 * D_MODEL)),
                  _whole((D_HYENA, D_MODEL)), _whole((D_CONV, D_MODEL)), _whole((D_MODEL, D_MODEL)),
                  _whole((1, D_MODEL)), _whole((1, D_MODEL)), _whole((D_MODEL, D_FF)),
                  _whole((D_MODEL, D_FF)), _whole((D_FF, D_MODEL)), _whole((1, D_MODEL))],
        out_specs=tile(D_MODEL),
        out_shape=jax.ShapeDtypeStruct((tokens, D_MODEL), F32),
        compiler_params=pltpu.CompilerParams(
            dimension_semantics=("arbitrary",), vmem_limit_bytes=VMEM_LIMIT_BYTES),
        name="mix_ffn",
    )(x1, z, x0, cact, g_mix, w_gates, hy_w_out, cv_w_out, w_out, g_mixpost, g_pre, wg, wu, wd, g_post)


def _encoder_layer(xs, p):
    row = lambda v: v.reshape(1, -1)
    bf = lambda w: w.astype(BF16)
    o2 = 3 * D_HYENA + 2 * D_CONV
    kr, ki = _filter_spectrum(p["hy_filt_w1"], p["hy_filt_b1"], p["hy_filt_freq1"], p["hy_filt_w2"],
                              p["hy_filt_b2"], p["hy_filt_freq2"], p["hy_filt_w3"])
    win_a = bf(p["w_in"][:, :o2])
    w_gates = bf(p["w_in"][:, o2:])
    ffn1 = (row(p["ffn1_norm_pre"]), bf(p["ffn1_w_gate"]), bf(p["ffn1_w_up"]), bf(p["ffn1_w_down"]),
            row(p["ffn1_norm_post"]))
    ffn2 = (row(p["ffn2_norm_pre"]), bf(p["ffn2_w_gate"]), bf(p["ffn2_w_up"]), bf(p["ffn2_w_down"]),
            row(p["ffn2_norm_post"]))
    g_mix = row(p["mix_norm_pre"])
    outs = []
    for x in xs:
        batch, seq, _ = x.shape
        assert seq == SEQ and (batch * seq) % TOKEN_TILE == 0
        x1, sig, x0, cact = _ffn_proj(
            x.reshape(batch * seq, D_MODEL), *ffn1, g_mix, win_a, p["hy_short_w"],
            row(p["hy_short_b"]), p["cv_dw_w"], row(p["cv_dw_b"]), row(p["cv_ln_g"]), row(p["cv_ln_b"]))
        z = _long_conv(sig.reshape(batch, seq, D_HYENA), p["hy_bias"], kr, ki)
        y = _mix_ffn(x1, z.reshape(batch * seq, D_HYENA), x0, cact, g_mix, w_gates,
                     bf(p["hy_w_out"]), bf(p["cv_w_out"]), bf(p["w_out"]), row(p["mix_norm_post"]), *ffn2)
        outs.append(y.reshape(batch, seq, D_MODEL))
    return outs


_PARAM_NAMES = (
    "ffn1_norm_pre", "ffn1_w_gate", "ffn1_w_up", "ffn1_w_down", "ffn1_norm_post",
    "mix_norm_pre", "w_in", "hy_short_w", "hy_short_b",
    "hy_filt_w1", "hy_filt_b1", "hy_filt_freq1", "hy_filt_w2", "hy_filt_b2", "hy_filt_freq2",
    "hy_filt_w3", "hy_bias", "hy_w_out", "cv_dw_w", "cv_dw_b", "cv_ln_g", "cv_ln_b", "cv_w_out",
    "w_out", "mix_norm_post",
    "ffn2_norm_pre", "ffn2_w_gate", "ffn2_w_up", "ffn2_w_down", "ffn2_norm_post")


def kernel(x_prompt, x_sample, ffn1_norm_pre, ffn1_w_gate, ffn1_w_up, ffn1_w_down, ffn1_norm_post, mix_norm_pre, w_in, hy_short_w, hy_short_b, hy_filt_w1, hy_filt_b1, hy_filt_freq1, hy_filt_w2, hy_filt_b2, hy_filt_freq2, hy_filt_w3, hy_bias, hy_w_out, cv_dw_w, cv_dw_b, cv_ln_g, cv_ln_b, cv_w_out, w_out, mix_norm_post, ffn2_norm_pre, ffn2_w_gate, ffn2_w_up, ffn2_w_down, ffn2_norm_post):
    stacked = dict(zip(_PARAM_NAMES, (
        ffn1_norm_pre, ffn1_w_gate, ffn1_w_up, ffn1_w_down, ffn1_norm_post,
        mix_norm_pre, w_in, hy_short_w, hy_short_b,
        hy_filt_w1, hy_filt_b1, hy_filt_freq1, hy_filt_w2, hy_filt_b2, hy_filt_freq2,
        hy_filt_w3, hy_bias, hy_w_out, cv_dw_w, cv_dw_b, cv_ln_g, cv_ln_b, cv_w_out,
        w_out, mix_norm_post,
        ffn2_norm_pre, ffn2_w_gate, ffn2_w_up, ffn2_w_down, ffn2_norm_post)))
    xs = [x_prompt, x_sample]
    for layer in range(ffn1_norm_pre.shape[0]):
        xs = _encoder_layer(xs, {k: v[layer] for k, v in stacked.items()})
    return (xs[0], xs[1])
```

```python
import functools
import math

import jax
import jax.numpy as jnp
import numpy as np
from jax import lax
from jax.experimental import pallas as pl
from jax.experimental.pallas import tpu as pltpu

D_MODEL = 1024
D_HYENA = 512
D_CONV = 512
D_FF = 2816
HYENA_SHORT = 3
CONF_KERNEL = 31
FILTER_EMB = 33
FILTER_BANDS = (FILTER_EMB - 1) // 2
FILTER_HIDDEN = 64
DECAY_TARGET = 1e-2
FAST_DECAY_PCT = 0.3
SLOW_DECAY_PCT = 1.5
EPS = 1e-6

F32 = jnp.float32
BF16 = jnp.bfloat16

LANES = 128
SUBLANES = 8
VMEM_LIMIT_BYTES = 60 * 1024 * 1024

SEQ = 8192
NFFT = 2 * SEQ
N2 = 128
N1 = NFFT // N2
T1 = SEQ // N2
F1P = 72
CHUNK = LANES
N_CHUNKS = D_HYENA // CHUNK
F1_GROUP = 8
N_GROUPS = F1P // F1_GROUP
GROUP_LANES = F1_GROUP * CHUNK
PITCH_A = 2 * F1P + SUBLANES
PITCH_B = 2 * N2 + SUBLANES
PITCH_Y = N2 + SUBLANES
S_ROWS = max(N2 * PITCH_A, F1P * PITCH_B)
T2_UNROLL = 16
FILTER_COLS = 1024

TOKEN_TILE = 512
CONV_ROWS = 128
CONV_HALO = 16
SHORT_HALO = 8
ROW_STEP = 2


@functools.lru_cache(maxsize=None)
def _fft_tables():
    t1 = np.arange(T1, dtype=np.int64)
    t2 = np.arange(N2, dtype=np.int64)
    f1 = np.arange(F1P, dtype=np.int64)
    keep = (f1 <= N1 // 2).astype(np.float64)
    k = (f1[None, :, None] * (N2 * t1[None, None, :] + t2[:, None, None])) % NFFT
    ph = 2.0 * np.pi * k.astype(np.float64) / NFFT
    fwd = np.concatenate([np.cos(ph) * keep[None, :, None],
                          -np.sin(ph) * keep[None, :, None]], axis=1)
    wgt = np.where((f1 == 0) | (f1 == N1 // 2), 1.0, 2.0) * keep / NFFT
    phq = np.transpose(ph, (0, 2, 1))
    inv = np.concatenate([np.cos(phq) * wgt[None, None, :],
                          -np.sin(phq) * wgt[None, None, :]], axis=2)
    a = np.arange(N2, dtype=np.int64)
    th = 2.0 * np.pi * ((a[:, None] * a[None, :]) % N2).astype(np.float64) / N2
    c, s = np.cos(th), np.sin(th)
    f2_fwd = np.block([[c, s], [-s, c]])
    f2_inv = np.block([[c, -s], [s, c]])
    f32 = np.float32
    return fwd.astype(f32), inv.astype(f32), f2_fwd.astype(f32), f2_inv.astype(f32)


def _filter_constants():
    bands = np.linspace(1e-4, FILTER_BANDS - 1, FILTER_BANDS).astype(np.float32)
    max_decay = math.log(DECAY_TARGET) / FAST_DECAY_PCT
    min_decay = math.log(DECAY_TARGET) / SLOW_DECAY_PCT
    deltas = np.abs(np.linspace(min_decay, max_decay, D_HYENA)).astype(np.float32)
    return bands.reshape(FILTER_BANDS, 1), deltas.reshape(1, D_HYENA)


def _whole(shape):
    zeros = (0,) * len(shape)
    return pl.BlockSpec(shape, lambda *_: zeros, pipeline_mode=pl.Buffered(1))


def _rms(x, g):
    return x * lax.rsqrt(jnp.mean(x * x, axis=-1, keepdims=True) + EPS) * g


def _dot(a, b):
    return jnp.dot(a, b, preferred_element_type=F32)


def _swiglu_half_step(x, g_pre, wg_ref, wu_ref, wd_ref, g_post):
    h = _rms(x, g_pre).astype(BF16)
    gate = _dot(h, wg_ref[...])
    up = _dot(h, wu_ref[...])
    act = (gate * jax.nn.sigmoid(gate) * up).astype(BF16)
    return x + 0.5 * _rms(_dot(act, wd_ref[...]), g_post)


def _t1_dft(load_signal, s_ref, fwd_ref):
    def body(t2, carry):
        a = _dot(fwd_ref[t2], load_signal(t2).astype(BF16))
        s_ref[pl.ds(pl.multiple_of(t2 * PITCH_A, SUBLANES), 2 * F1P), :] = a
        return carry
    lax.fori_loop(0, N2, body, 0, unroll=T2_UNROLL)


def _gather_t2_major(s_ref, r_ref):
    def body(g, carry):
        for k in range(F1_GROUP):
            f1 = g * F1_GROUP + k
            re = s_ref[pl.ds(f1, N2, stride=PITCH_A), :]
            im = s_ref[pl.ds(F1P + f1, N2, stride=PITCH_A), :]
            r_ref[g, 0:N2, k * CHUNK:(k + 1) * CHUNK] = re.astype(BF16)
            r_ref[g, N2:2 * N2, k * CHUNK:(k + 1) * CHUNK] = im.astype(BF16)
        return carry
    lax.fori_loop(0, N_GROUPS, body, 0)


def _forward_spectrum_rows(load_signal, s_ref, r_ref, fwd_ref):
    _t1_dft(load_signal, s_ref, fwd_ref)
    _gather_t2_major(s_ref, r_ref)


def _time_major_rows(w_ref):
    return lambda t2: w_ref[pl.ds(t2, T1, stride=PITCH_Y), :]


def _signal_rows(t1):
    return pl.ds(pl.multiple_of(t1 * PITCH_Y, SUBLANES), N2)


def _filter_kernel(w1t_ref, w1c_ref, w1s_ref, b1_ref, fr1_ref, w2_ref, b2_ref, fr2_ref,
                   w3f_ref, w3b_ref, bands_ref, delta_ref, fwd_ref, f2f_ref,
                   kr_ref, ki_ref, h_ref, w_ref, s_ref, rf_ref, rb_ref):
    hi = lax.Precision.HIGHEST

    def dot_hi(a, b):
        return jnp.dot(a, b, preferred_element_type=F32, precision=hi)

    @pl.when(pl.program_id(0) == 0)
    def _():
        def body(i, carry):
            col0 = pl.multiple_of(i * FILTER_COLS, FILTER_COLS)
            n = (col0 + lax.broadcasted_iota(jnp.int32, (1, FILTER_COLS), 1)).astype(F32)
            t = n * (1.0 / (SEQ - 1))
            ang = bands_ref[...] * (n * (2.0 * math.pi / SEQ))
            zw = (w1t_ref[...] * t + dot_hi(w1c_ref[...], jnp.cos(ang))
                  - dot_hi(w1s_ref[...], jnp.sin(ang)))
            h = jnp.sin(fr1_ref[...] * (zw + b1_ref[...]))
            h = jnp.sin(fr2_ref[...] * (dot_hi(w2_ref[...], h) + b2_ref[...]))
            for k in range(FILTER_COLS // N2):
                h_ref[i * (FILTER_COLS // N2) + k] = h[:, k * N2:(k + 1) * N2]
            return carry
        lax.fori_loop(0, SEQ // FILTER_COLS, body, 0)

    def fill(w3_ref):
        def body(t1, ss):
            row0 = pl.multiple_of(t1 * N2, N2)
            n = (row0 + lax.broadcasted_iota(jnp.int32, (N2, 1), 0)).astype(F32)
            decay = jnp.exp(-(n * (1.0 / (SEQ - 1))) * delta_ref[...])
            rows = dot_hi(h_ref[t1].T, w3_ref[...]) * decay
            w_ref[_signal_rows(t1), :] = rows
            return ss + jnp.sum(rows * rows, axis=0, keepdims=True)
        return lax.fori_loop(0, T1, body, jnp.zeros((1, CHUNK), F32), unroll=4)

    ss_f = fill(w3f_ref)
    f0 = w_ref[0:1, :]
    _forward_spectrum_rows(_time_major_rows(w_ref), s_ref, rf_ref, fwd_ref)
    ss_b = fill(w3b_ref)
    b0 = w_ref[0:1, :]
    _forward_spectrum_rows(_time_major_rows(w_ref), s_ref, rb_ref, fwd_ref)
    scale = lax.rsqrt(ss_f + ss_b + 2.0 * f0 * b0 + EPS)
    scale = jnp.concatenate([scale] * F1_GROUP, axis=1)

    def body(g, carry):
        xf = _dot(f2f_ref[...], rf_ref[g])
        xb = _dot(f2f_ref[...], rb_ref[g])
        kr_ref[0, g] = scale * (xf[0:N2] + xb[0:N2])
        ki_ref[0, g] = scale * (xf[N2:2 * N2] - xb[N2:2 * N2])
        return carry
    lax.fori_loop(0, N_GROUPS, body, 0)


def _filter_spectrum(w1, b1, fr1, w2, b2, fr2, w3):
    fwd, _, f2f, _ = _fft_tables()
    bands, deltas = _filter_constants()
    col = lambda v: v.reshape(-1, 1)
    small = lambda shape: pl.BlockSpec(shape, lambda c: (0,) * len(shape))
    spec_out = pl.BlockSpec((1, N_GROUPS, N2, GROUP_LANES), lambda c: (c, 0, 0, 0))
    out_shape = jax.ShapeDtypeStruct((N_CHUNKS, N_GROUPS, N2, GROUP_LANES), F32)
    hid = FILTER_HIDDEN
    return pl.pallas_call(
        _filter_kernel,
        grid=(N_CHUNKS,),
        in_specs=[
            small((hid, 1)), small((hid, FILTER_BANDS)), small((hid, FILTER_BANDS)),
            small((hid, 1)), small((hid, 1)), small((hid, hid)), small((hid, 1)), small((hid, 1)),
            pl.BlockSpec((hid, CHUNK), lambda c: (0, c)),
            pl.BlockSpec((hid, CHUNK), lambda c: (0, N_CHUNKS + c)),
            small((FILTER_BANDS, 1)),
            pl.BlockSpec((1, CHUNK), lambda c: (0, c)),
            _whole((N2, 2 * F1P, T1)), _whole((2 * N2, 2 * N2)),
        ],
        out_specs=[spec_out, spec_out],
        out_shape=[out_shape, out_shape],
        scratch_shapes=[
            pltpu.VMEM((T1, hid, N2), F32),
            pltpu.VMEM((T1 * PITCH_Y, CHUNK), F32),
            pltpu.VMEM((S_ROWS, CHUNK), F32),
            pltpu.VMEM((N_GROUPS, 2 * N2, GROUP_LANES), BF16),
            pltpu.VMEM((N_GROUPS, 2 * N2, GROUP_LANES), BF16),
        ],
        compiler_params=pltpu.CompilerParams(
            dimension_semantics=("arbitrary",), vmem_limit_bytes=VMEM_LIMIT_BYTES),
        name="filter_spectrum",
    )(col(w1[0]), w1[1:1 + FILTER_BANDS].T, w1[1 + FILTER_BANDS:].T, col(b1), col(fr1), w2.T,
      col(b2), col(fr2), w3, w3, jnp.asarray(bands), jnp.asarray(deltas),
      jnp.asarray(fwd).astype(BF16), jnp.asarray(f2f).astype(BF16))


def _ffn_proj_kernel(x_ref, g_pre_ref, wg_ref, wu_ref, wd_ref, g_post_ref, g_mix_ref, win_ref,
                     sw_ref, sb_ref, dww_ref, dwb_ref, lng_ref, lnb_ref,
                     x1_ref, sig_ref, x0_ref, cact_ref, hy_ext, glu_ext, hy_new, glu_new):
    tm = TOKEN_TILE
    hs, hc = SHORT_HALO, CONV_HALO
    step = pl.program_id(0)
    tiles_per_seq = SEQ // tm
    new_starts_seq = (step + tiles_per_seq - 1) % tiles_per_seq == 0
    cur_starts_seq = step % tiles_per_seq == 0

    def rows(start, size):
        return pl.ds(ROW_STEP * start, size, stride=ROW_STEP)

    def lanes(slab):
        return slice(slab * LANES, (slab + 1) * LANES)

    @pl.when(step == 0)
    def _():
        hy_ext[...] = jnp.zeros_like(hy_ext)
        glu_ext[...] = jnp.zeros_like(glu_ext)
        hy_new[...] = jnp.zeros_like(hy_new)
        glu_new[...] = jnp.zeros_like(glu_new)

    n_h = D_HYENA // LANES
    for r0 in range(0, tm, CONV_ROWS):
        def depthwise(ext, slab, halo, w_ref, b_ref, taps):
            acc = b_ref[:, lanes(slab)]
            for j in range(taps):
                acc = acc + (w_ref[j:j + 1, lanes(slab)]
                             * ext[slab, rows(r0 + halo + j - taps // 2, CONV_ROWS), :])
            return acc

        out_rows = slice(r0, r0 + CONV_ROWS)
        for slab in range(n_h):
            x0_ref[out_rows, lanes(slab)] = depthwise(hy_ext, slab, hs, sw_ref, sb_ref, HYENA_SHORT)
            sig_ref[out_rows, lanes(slab)] = (
                depthwise(hy_ext, 2 * n_h + slab, hs, sw_ref, sb_ref, HYENA_SHORT)
                * depthwise(hy_ext, n_h + slab, hs, sw_ref, sb_ref, HYENA_SHORT))
        c = jnp.concatenate([depthwise(glu_ext, slab, hc, dww_ref, dwb_ref, CONF_KERNEL)
                             for slab in range(D_CONV // LANES)], axis=1)
        mu = jnp.mean(c, axis=-1, keepdims=True)
        cc = c - mu
        c = cc * lax.rsqrt(jnp.mean(cc * cc, axis=-1, keepdims=True) + EPS) * lng_ref[...] + lnb_ref[...]
        cact_ref[out_rows, :] = (c * jax.nn.sigmoid(c)).astype(BF16)

    x1 = _swiglu_half_step(x_ref[...], g_pre_ref[...], wg_ref, wu_ref, wd_ref, g_post_ref[...])
    x1_ref[...] = x1
    u = _rms(x1, g_mix_ref[...]).astype(BF16)
    proj = _dot(u, win_ref[...])
    o1 = 3 * D_HYENA
    hy = proj[:, :o1]
    glu = proj[:, o1:o1 + D_CONV] * jax.nn.sigmoid(proj[:, o1 + D_CONV:])

    for ext, new, cur, h in ((hy_ext, hy_new, hy, hs), (glu_ext, glu_new, glu, hc)):
        for slab in range(cur.shape[1] // LANES):
            tail = ext[slab, rows(tm, h), :]
            ext[slab, rows(0, h), :] = jnp.where(new_starts_seq, 0.0, tail)
            ext[slab, rows(h, tm), :] = new[:, lanes(slab)]
            ext[slab, rows(h + tm, h), :] = jnp.where(cur_starts_seq, 0.0, cur[0:h, lanes(slab)])
        new[...] = cur


def _ffn_proj(x, g_pre, wg, wu, wd, g_post, g_mix, win_a, short_w, short_b, dw_w, dw_b, ln_g, ln_b):
    tokens = x.shape[0]
    tm = TOKEN_TILE
    n = tokens // tm
    cur = lambda width: pl.BlockSpec((tm, width), lambda s: (jnp.minimum(s, n - 1), 0))
    lag = lambda width: pl.BlockSpec((tm, width), lambda s: (jnp.maximum(s - 2, 0), 0))
    n_a = win_a.shape[1]
    return pl.pallas_call(
        _ffn_proj_kernel,
        grid=(n + 2,),
        in_specs=[cur(D_MODEL), _whole((1, D_MODEL)), _whole((D_MODEL, D_FF)),
                  _whole((D_MODEL, D_FF)), _whole((D_FF, D_MODEL)), _whole((1, D_MODEL)),
                  _whole((1, D_MODEL)), _whole((D_MODEL, n_a)),
                  _whole((HYENA_SHORT, 3 * D_HYENA)), _whole((1, 3 * D_HYENA)),
                  _whole((CONF_KERNEL, D_CONV)), _whole((1, D_CONV)), _whole((1, D_CONV)),
                  _whole((1, D_CONV))],
        out_specs=[cur(D_MODEL), lag(D_HYENA), lag(D_HYENA), lag(D_CONV)],
        out_shape=[jax.ShapeDtypeStruct((tokens, D_MODEL), F32),
                   jax.ShapeDtypeStruct((tokens, D_HYENA), F32),
                   jax.ShapeDtypeStruct((tokens, D_HYENA), F32),
                   jax.ShapeDtypeStruct((tokens, D_CONV), BF16)],
        scratch_shapes=[pltpu.VMEM((3 * D_HYENA // LANES, ROW_STEP * (tm + 2 * SHORT_HALO), LANES), F32),
                        pltpu.VMEM((D_CONV // LANES, ROW_STEP * (tm + 2 * CONV_HALO), LANES), F32),
                        pltpu.VMEM((tm, 3 * D_HYENA), F32),
                        pltpu.VMEM((tm, D_CONV), F32)],
        compiler_params=pltpu.CompilerParams(
            dimension_semantics=("arbitrary",), vmem_limit_bytes=VMEM_LIMIT_BYTES),
        name="ffn_proj",
    )(x, g_pre, wg, wu, wd, g_post, g_mix, win_a, short_w, short_b, dw_w, dw_b, ln_g, ln_b)


def _long_conv_kernel(sig_ref, hb_ref, fwd_ref, inv_ref, f2f_ref, f2i_ref, kr_ref, ki_ref,
                      z_ref, s_ref, r_ref, w_ref):
    def in_body(t1, carry):
        w_ref[_signal_rows(t1), :] = sig_ref[0, pl.ds(pl.multiple_of(t1 * N2, N2), N2), :]
        return carry
    lax.fori_loop(0, T1, in_body, 0, unroll=4)
    _forward_spectrum_rows(_time_major_rows(w_ref), s_ref, r_ref, fwd_ref)

    def freq_body(g, carry):
        x = _dot(f2f_ref[...], r_ref[g])
        xr, xi = x[0:N2], x[N2:2 * N2]
        kr, ki = kr_ref[0, g], ki_ref[0, g]
        y = jnp.concatenate([xr * kr - xi * ki, xr * ki + xi * kr], axis=0).astype(BF16)
        b = _dot(f2i_ref[...], y)
        for k in range(F1_GROUP):
            row0 = pl.multiple_of((g * F1_GROUP + k) * PITCH_B, SUBLANES)
            s_ref[pl.ds(row0, 2 * N2), :] = b[:, k * CHUNK:(k + 1) * CHUNK]
        return carry
    lax.fori_loop(0, N_GROUPS, freq_body, 0, unroll=3)

    def inv_body(t2, carry):
        g = jnp.concatenate([s_ref[pl.ds(t2, F1P, stride=PITCH_B), :],
                             s_ref[pl.ds(N2 + t2, F1P, stride=PITCH_B), :]], axis=0)
        w_ref[pl.ds(t2, T1, stride=PITCH_Y), :] = _dot(inv_ref[t2], g.astype(BF16))
        return carry
    lax.fori_loop(0, N2, inv_body, 0, unroll=T2_UNROLL)

    def out_body(t1, carry):
        rows = pl.ds(pl.multiple_of(t1 * N2, N2), N2)
        z_ref[0, rows, :] = w_ref[_signal_rows(t1), :] + sig_ref[0, rows, :] * hb_ref[...]
        return carry
    lax.fori_loop(0, T1, out_body, 0, unroll=4)


def _long_conv(sig, hy_bias, kr, ki):
    batch = sig.shape[0]
    fwd, inv, f2f, f2i = _fft_tables()
    spec_k = pl.BlockSpec((1, N_GROUPS, N2, GROUP_LANES), lambda c, b: (c, 0, 0, 0),
                          pipeline_mode=pl.Buffered(1))
    spec_io = pl.BlockSpec((1, SEQ, CHUNK), lambda c, b: (b, 0, c))
    return pl.pallas_call(
        _long_conv_kernel,
        grid=(N_CHUNKS, batch),
        in_specs=[spec_io, pl.BlockSpec((1, CHUNK), lambda c, b: (0, c)),
                  _whole((N2, 2 * F1P, T1)), _whole((N2, T1, 2 * F1P)),
                  _whole((2 * N2, 2 * N2)), _whole((2 * N2, 2 * N2)), spec_k, spec_k],
        out_specs=spec_io,
        out_shape=jax.ShapeDtypeStruct((batch, SEQ, D_HYENA), F32),
        scratch_shapes=[
            pltpu.VMEM((S_ROWS, CHUNK), F32),
            pltpu.VMEM((N_GROUPS, 2 * N2, GROUP_LANES), BF16),
            pltpu.VMEM((T1 * PITCH_Y, CHUNK), F32),
        ],
        compiler_params=pltpu.CompilerParams(
            dimension_semantics=("arbitrary", "arbitrary"), vmem_limit_bytes=VMEM_LIMIT_BYTES),
        name="long_conv",
    )(sig, hy_bias.reshape(1, -1),
      jnp.asarray(fwd).astype(BF16), jnp.asarray(inv).astype(BF16),
      jnp.asarray(f2f).astype(BF16), jnp.asarray(f2i).astype(BF16), kr, ki)


def _mix_ffn_kernel(x1_ref, z_ref, x0_ref, cact_ref, g_mix_ref, wgate_ref, hyw_ref, cvw_ref,
                    wout_ref, g_mixpost_ref, g_pre_ref, wg_ref, wu_ref, wd_ref, g_post_ref,
                    out_ref):
    y_a = _dot((z_ref[...] * x0_ref[...]).astype(BF16), hyw_ref[...])
    y_b = _dot(cact_ref[...], cvw_ref[...])
    x1 = x1_ref[...]
    u = _rms(x1, g_mix_ref[...]).astype(BF16)
    gates = _dot(u, wgate_ref[...])
    merged = (jax.nn.sigmoid(gates[:, :D_MODEL]) * y_a + jax.nn.sigmoid(gates[:, D_MODEL:]) * y_b)
    m = _dot(merged.astype(BF16), wout_ref[...])
    x2 = x1 + _rms(m, g_mixpost_ref[...])
    out_ref[...] = _swiglu_half_step(x2, g_pre_ref[...], wg_ref, wu_ref, wd_ref, g_post_ref[...])


def _mix_ffn(x1, z, x0, cact, g_mix, w_gates, hy_w_out, cv_w_out, w_out, g_mixpost,
             g_pre, wg, wu, wd, g_post):
    tokens = x1.shape[0]
    tm = TOKEN_TILE
    tile = lambda width: pl.BlockSpec((tm, width), lambda i: (i, 0))
    return pl.pallas_call(
        _mix_ffn_kernel,
        grid=(tokens // tm,),
        in_specs=[tile(D_MODEL), tile(D_HYENA), tile(D_HYENA), tile(D_CONV),
                  _whole((1, D_MODEL)), _whole((D_MODEL, 2 * D_MODEL)),
                  _whole((D_HYENA, D_MODEL)), _whole((D_CONV, D_MODEL)), _whole((D_MODEL, D_MODEL)),
                  _whole((1, D_MODEL)), _whole((1, D_MODEL)), _whole((D_MODEL, D_FF)),
                  _whole((D_MODEL, D_FF)), _whole((D_FF, D_MODEL)), _whole((1, D_MODEL))],
        out_specs=tile(D_MODEL),
        out_shape=jax.ShapeDtypeStruct((tokens, D_MODEL), F32),
        compiler_params=pltpu.CompilerParams(
            dimension_semantics=("arbitrary",), vmem_limit_bytes=VMEM_LIMIT_BYTES),
        name="mix_ffn",
    )(x1, z, x0, cact, g_mix, w_gates, hy_w_out, cv_w_out, w_out, g_mixpost, g_pre, wg, wu, wd, g_post)


def _encoder_layer(xs, p):
    row = lambda v: v.reshape(1, -1)
    bf = lambda w: w.astype(BF16)
    o2 = 3 * D_HYENA + 2 * D_CONV
    kr, ki = _filter_spectrum(p["hy_filt_w1"], p["hy_filt_b1"], p["hy_filt_freq1"], p["hy_filt_w2"],
                              p["hy_filt_b2"], p["hy_filt_freq2"], p["hy_filt_w3"])
    win_a = bf(p["w_in"][:, :o2])
    w_gates = bf(p["w_in"][:, o2:])
    ffn1 = (row(p["ffn1_norm_pre"]), bf(p["ffn1_w_gate"]), bf(p["ffn1_w_up"]), bf(p["ffn1_w_down"]),
            row(p["ffn1_norm_post"]))
    ffn2 = (row(p["ffn2_norm_pre"]), bf(p["ffn2_w_gate"]), bf(p["ffn2_w_up"]), bf(p["ffn2_w_down"]),
            row(p["ffn2_norm_post"]))
    g_mix = row(p["mix_norm_pre"])
    outs = []
    for x in xs:
        batch, seq, _ = x.shape
        assert seq == SEQ and (batch * seq) % TOKEN_TILE == 0
        x1, sig, x0, cact = _ffn_proj(
            x.reshape(batch * seq, D_MODEL), *ffn1, g_mix, win_a, p["hy_short_w"],
            row(p["hy_short_b"]), p["cv_dw_w"], row(p["cv_dw_b"]), row(p["cv_ln_g"]), row(p["cv_ln_b"]))
        z = _long_conv(sig.reshape(batch, seq, D_HYENA), p["hy_bias"], kr, ki)
        y = _mix_ffn(x1, z.reshape(batch * seq, D_HYENA), x0, cact, g_mix, w_gates,
                     bf(p["hy_w_out"]), bf(p["cv_w_out"]), bf(p["w_out"]), row(p["mix_norm_post"]), *ffn2)
        outs.append(y.reshape(batch, seq, D_MODEL))
    return outs


_PARAM_NAMES = (
    "ffn1_norm_pre", "ffn1_w_gate", "ffn1_w_up", "ffn1_w_down", "ffn1_norm_post",
    "mix_norm_pre", "w_in", "hy_short_w", "hy_short_b",
    "hy_filt_w1", "hy_filt_b1", "hy_filt_freq1", "hy_filt_w2", "hy_filt_b2", "hy_filt_freq2",
    "hy_filt_w3", "hy_bias", "hy_w_out", "cv_dw_w", "cv_dw_b", "cv_ln_g", "cv_ln_b", "cv_w_out",
    "w_out", "mix_norm_post",
    "ffn2_norm_pre", "ffn2_w_gate", "ffn2_w_up", "ffn2_w_down", "ffn2_norm_post")


def kernel(x_prompt, x_sample, ffn1_norm_pre, ffn1_w_gate, ffn1_w_up, ffn1_w_down, ffn1_norm_post, mix_norm_pre, w_in, hy_short_w, hy_short_b, hy_filt_w1, hy_filt_b1, hy_filt_freq1, hy_filt_w2, hy_filt_b2, hy_filt_freq2, hy_filt_w3, hy_bias, hy_w_out, cv_dw_w, cv_dw_b, cv_ln_g, cv_ln_b, cv_w_out, w_out, mix_norm_post, ffn2_norm_pre, ffn2_w_gate, ffn2_w_up, ffn2_w_down, ffn2_norm_post):
    stacked = dict(zip(_PARAM_NAMES, (
        ffn1_norm_pre, ffn1_w_gate, ffn1_w_up, ffn1_w_down, ffn1_norm_post,
        mix_norm_pre, w_in, hy_short_w, hy_short_b,
        hy_filt_w1, hy_filt_b1, hy_filt_freq1, hy_filt_w2, hy_filt_b2, hy_filt_freq2,
        hy_filt_w3, hy_bias, hy_w_out, cv_dw_w, cv_dw_b, cv_ln_g, cv_ln_b, cv_w_out,
        w_out, mix_norm_post,
        ffn2_norm_pre, ffn2_w_gate, ffn2_w_up, ffn2_w_down, ffn2_norm_post)))
    xs = [x_prompt, x_sample]
    for layer in range(ffn1_norm_pre.shape[0]):
        xs = _encoder_layer(xs, {k: v[layer] for k, v in stacked.items()})
    return (xs[0], xs[1])
```

```python
import functools
import math

import jax
import jax.numpy as jnp
import numpy as np
from jax import lax
from jax.experimental import pallas as pl
from jax.experimental.pallas import tpu as pltpu

D_MODEL = 1024
D_HYENA = 512
D_CONV = 512
D_FF = 2816
HYENA_SHORT = 3
CONF_KERNEL = 31
FILTER_EMB = 33
FILTER_BANDS = (FILTER_EMB - 1) // 2
FILTER_HIDDEN = 64
DECAY_TARGET = 1e-2
FAST_DECAY_PCT = 0.3
SLOW_DECAY_PCT = 1.5
EPS = 1e-6

F32 = jnp.float32
BF16 = jnp.bfloat16

LANES = 128
SUBLANES = 8
VMEM_LIMIT_BYTES = 60 * 1024 * 1024

SEQ = 8192
NFFT = 2 * SEQ
N2 = 128
N1 = NFFT // N2
T1 = SEQ // N2
F1P = 72
CHUNK = LANES
N_CHUNKS = D_HYENA // CHUNK
F1_GROUP = 8
N_GROUPS = F1P // F1_GROUP
GROUP_LANES = F1_GROUP * CHUNK
PITCH_A = 2 * F1P + SUBLANES
PITCH_B = 2 * N2 + SUBLANES
PITCH_Y = N2 + SUBLANES
S_ROWS = max(N2 * PITCH_A, F1P * PITCH_B)
T2_UNROLL = 32
FILTER_COLS = 1024

TOKEN_TILE = 512
CONV_ROWS = 32
CONV_HALO = 16
SHORT_HALO = 8
ROW_STEP = 2


@functools.lru_cache(maxsize=None)
def _fft_tables():
    t1 = np.arange(T1, dtype=np.int64)
    t2 = np.arange(N2, dtype=np.int64)
    f1 = np.arange(F1P, dtype=np.int64)
    keep = (f1 <= N1 // 2).astype(np.float64)
    k = (f1[None, :, None] * (N2 * t1[None, None, :] + t2[:, None, None])) % NFFT
    ph = 2.0 * np.pi * k.astype(np.float64) / NFFT
    fwd = np.concatenate([np.cos(ph) * keep[None, :, None],
                          -np.sin(ph) * keep[None, :, None]], axis=1)
    wgt = np.where((f1 == 0) | (f1 == N1 // 2), 1.0, 2.0) * keep / NFFT
    phq = np.transpose(ph, (0, 2, 1))
    inv = np.concatenate([np.cos(phq) * wgt[None, None, :],
                          -np.sin(phq) * wgt[None, None, :]], axis=2)
    a = np.arange(N2, dtype=np.int64)
    th = 2.0 * np.pi * ((a[:, None] * a[None, :]) % N2).astype(np.float64) / N2
    c, s = np.cos(th), np.sin(th)
    f2_fwd = np.block([[c, s], [-s, c]])
    f2_inv = np.block([[c, -s], [s, c]])
    f32 = np.float32
    return fwd.astype(f32), inv.astype(f32), f2_fwd.astype(f32), f2_inv.astype(f32)


def _filter_constants():
    bands = np.linspace(1e-4, FILTER_BANDS - 1, FILTER_BANDS).astype(np.float32)
    max_decay = math.log(DECAY_TARGET) / FAST_DECAY_PCT
    min_decay = math.log(DECAY_TARGET) / SLOW_DECAY_PCT
    deltas = np.abs(np.linspace(min_decay, max_decay, D_HYENA)).astype(np.float32)
    return bands.reshape(FILTER_BANDS, 1), deltas.reshape(1, D_HYENA)


def _whole(shape):
    zeros = (0,) * len(shape)
    return pl.BlockSpec(shape, lambda *_: zeros, pipeline_mode=pl.Buffered(1))


def _rms(x, g):
    return x * lax.rsqrt(jnp.mean(x * x, axis=-1, keepdims=True) + EPS) * g


def _dot(a, b):
    return jnp.dot(a, b, preferred_element_type=F32)


def _sigmoid(x):
    return 0.5 * jnp.tanh(0.5 * x) + 0.5


def _silu(x):
    half = 0.5 * x
    return half * jnp.tanh(half) + half


def _swiglu_half_step(x, g_pre, wg_ref, wu_ref, wd_ref, g_post):
    h = _rms(x, g_pre).astype(BF16)
    gate = _dot(h, wg_ref[...])
    up = _dot(h, wu_ref[...])
    act = (_silu(gate) * up).astype(BF16)
    return x + 0.5 * _rms(_dot(act, wd_ref[...]), g_post)


def _t1_dft(load_signal, s_ref, fwd_ref):
    def body(t2, carry):
        a = _dot(fwd_ref[t2], load_signal(t2).astype(BF16))
        s_ref[pl.ds(pl.multiple_of(t2 * PITCH_A, SUBLANES), 2 * F1P), :] = a
        return carry
    lax.fori_loop(0, N2, body, 0, unroll=T2_UNROLL)


def _gather_t2_major(s_ref, r_ref):
    def body(g, carry):
        for k in range(F1_GROUP):
            f1 = g * F1_GROUP + k
            re = s_ref[pl.ds(f1, N2, stride=PITCH_A), :]
            im = s_ref[pl.ds(F1P + f1, N2, stride=PITCH_A), :]
            r_ref[g, 0:N2, k * CHUNK:(k + 1) * CHUNK] = re.astype(BF16)
            r_ref[g, N2:2 * N2, k * CHUNK:(k + 1) * CHUNK] = im.astype(BF16)
        return carry
    lax.fori_loop(0, N_GROUPS, body, 0)


def _forward_spectrum_rows(load_signal, s_ref, r_ref, fwd_ref):
    _t1_dft(load_signal, s_ref, fwd_ref)
    _gather_t2_major(s_ref, r_ref)


def _time_major_rows(w_ref):
    return lambda t2: w_ref[pl.ds(t2, T1, stride=PITCH_Y), :]


def _signal_rows(t1):
    return pl.ds(pl.multiple_of(t1 * PITCH_Y, SUBLANES), N2)


def _filter_kernel(w1t_ref, w1c_ref, w1s_ref, b1_ref, fr1_ref, w2_ref, b2_ref, fr2_ref,
                   w3f_ref, w3b_ref, bands_ref, delta_ref, fwd_ref, f2f_ref,
                   kr_ref, ki_ref, h_ref, w_ref, s_ref, rf_ref, rb_ref):
    hi = lax.Precision.HIGHEST

    def dot_hi(a, b):
        return jnp.dot(a, b, preferred_element_type=F32, precision=hi)

    @pl.when(pl.program_id(0) == 0)
    def _():
        def body(i, carry):
            col0 = pl.multiple_of(i * FILTER_COLS, FILTER_COLS)
            n = (col0 + lax.broadcasted_iota(jnp.int32, (1, FILTER_COLS), 1)).astype(F32)
            t = n * (1.0 / (SEQ - 1))
            ang = bands_ref[...] * (n * (2.0 * math.pi / SEQ))
            zw = (w1t_ref[...] * t + dot_hi(w1c_ref[...], jnp.cos(ang))
                  - dot_hi(w1s_ref[...], jnp.sin(ang)))
            h = jnp.sin(fr1_ref[...] * (zw + b1_ref[...]))
            h = jnp.sin(fr2_ref[...] * (dot_hi(w2_ref[...], h) + b2_ref[...]))
            for k in range(FILTER_COLS // N2):
                h_ref[i * (FILTER_COLS // N2) + k] = h[:, k * N2:(k + 1) * N2]
            return carry
        lax.fori_loop(0, SEQ // FILTER_COLS, body, 0)

    def fill(w3_ref):
        def body(t1, ss):
            row0 = pl.multiple_of(t1 * N2, N2)
            n = (row0 + lax.broadcasted_iota(jnp.int32, (N2, 1), 0)).astype(F32)
            decay = jnp.exp(-(n * (1.0 / (SEQ - 1))) * delta_ref[...])
            rows = dot_hi(h_ref[t1].T, w3_ref[...]) * decay
            w_ref[_signal_rows(t1), :] = rows
            return ss + jnp.sum(rows * rows, axis=0, keepdims=True)
        return lax.fori_loop(0, T1, body, jnp.zeros((1, CHUNK), F32), unroll=4)

    ss_f = fill(w3f_ref)
    f0 = w_ref[0:1, :]
    _forward_spectrum_rows(_time_major_rows(w_ref), s_ref, rf_ref, fwd_ref)
    ss_b = fill(w3b_ref)
    b0 = w_ref[0:1, :]
    _forward_spectrum_rows(_time_major_rows(w_ref), s_ref, rb_ref, fwd_ref)
    scale = lax.rsqrt(ss_f + ss_b + 2.0 * f0 * b0 + EPS)
    scale = jnp.concatenate([scale] * F1_GROUP, axis=1)

    def body(g, carry):
        xf = _dot(f2f_ref[...], rf_ref[g])
        xb = _dot(f2f_ref[...], rb_ref[g])
        kr_ref[0, g] = scale * (xf[0:N2] + xb[0:N2])
        ki_ref[0, g] = scale * (xf[N2:2 * N2] - xb[N2:2 * N2])
        return carry
    lax.fori_loop(0, N_GROUPS, body, 0)


def _filter_spectrum(w1, b1, fr1, w2, b2, fr2, w3):
    fwd, _, f2f, _ = _fft_tables()
    bands, deltas = _filter_constants()
    col = lambda v: v.reshape(-1, 1)
    small = lambda shape: pl.BlockSpec(shape, lambda c: (0,) * len(shape))
    spec_out = pl.BlockSpec((1, N_GROUPS, N2, GROUP_LANES), lambda c: (c, 0, 0, 0))
    out_shape = jax.ShapeDtypeStruct((N_CHUNKS, N_GROUPS, N2, GROUP_LANES), F32)
    hid = FILTER_HIDDEN
    return pl.pallas_call(
        _filter_kernel,
        grid=(N_CHUNKS,),
        in_specs=[
            small((hid, 1)), small((hid, FILTER_BANDS)), small((hid, FILTER_BANDS)),
            small((hid, 1)), small((hid, 1)), small((hid, hid)), small((hid, 1)), small((hid, 1)),
            pl.BlockSpec((hid, CHUNK), lambda c: (0, c)),
            pl.BlockSpec((hid, CHUNK), lambda c: (0, N_CHUNKS + c)),
            small((FILTER_BANDS, 1)),
            pl.BlockSpec((1, CHUNK), lambda c: (0, c)),
            _whole((N2, 2 * F1P, T1)), _whole((2 * N2, 2 * N2)),
        ],
        out_specs=[spec_out, spec_out],
        out_shape=[out_shape, out_shape],
        scratch_shapes=[
            pltpu.VMEM((T1, hid, N2), F32),
            pltpu.VMEM((T1 * PITCH_Y, CHUNK), F32),
            pltpu.VMEM((S_ROWS, CHUNK), F32),
            pltpu.VMEM((N_GROUPS, 2 * N2, GROUP_LANES), BF16),
            pltpu.VMEM((N_GROUPS, 2 * N2, GROUP_LANES), BF16),
        ],
        compiler_params=pltpu.CompilerParams(
            dimension_semantics=("arbitrary",), vmem_limit_bytes=VMEM_LIMIT_BYTES),
        name="filter_spectrum",
    )(col(w1[0]), w1[1:1 + FILTER_BANDS].T, w1[1 + FILTER_BANDS:].T, col(b1), col(fr1), w2.T,
      col(b2), col(fr2), w3, w3, jnp.asarray(bands), jnp.asarray(deltas),
      jnp.asarray(fwd).astype(BF16), jnp.asarray(f2f).astype(BF16))


def _ffn_proj_kernel(x_ref, g_pre_ref, wg_ref, wu_ref, wd_ref, g_post_ref, g_mix_ref, win_ref,
                     sw_ref, sb_ref, dww_ref, dwb_ref, lng_ref, lnb_ref,
                     x1_ref, sig_ref, x0_ref, cact_ref, hy_ext, glu_ext, hy_new, glu_new):
    tm = TOKEN_TILE
    hs, hc = SHORT_HALO, CONV_HALO
    step = pl.program_id(0)
    tiles_per_seq = SEQ // tm
    new_starts_seq = (step + tiles_per_seq - 1) % tiles_per_seq == 0
    cur_starts_seq = step % tiles_per_seq == 0

    def rows(start, size):
        return pl.ds(ROW_STEP * start, size, stride=ROW_STEP)

    def lanes(slab):
        return slice(slab * LANES, (slab + 1) * LANES)

    @pl.when(step == 0)
    def _():
        hy_ext[...] = jnp.zeros_like(hy_ext)
        glu_ext[...] = jnp.zeros_like(glu_ext)
        hy_new[...] = jnp.zeros_like(hy_new)
        glu_new[...] = jnp.zeros_like(glu_new)

    n_h = D_HYENA // LANES
    for r0 in range(0, tm, CONV_ROWS):
        def depthwise(ext, slab, halo, w_ref, b_ref, taps):
            acc = b_ref[:, lanes(slab)]
            for j in range(taps):
                acc = acc + (w_ref[j:j + 1, lanes(slab)]
                             * ext[slab, rows(r0 + halo + j - taps // 2, CONV_ROWS), :])
            return acc

        out_rows = slice(r0, r0 + CONV_ROWS)
        for slab in range(n_h):
            x0_ref[out_rows, lanes(slab)] = depthwise(hy_ext, slab, hs, sw_ref, sb_ref, HYENA_SHORT)
            sig_ref[out_rows, lanes(slab)] = (
                depthwise(hy_ext, 2 * n_h + slab, hs, sw_ref, sb_ref, HYENA_SHORT)
                * depthwise(hy_ext, n_h + slab, hs, sw_ref, sb_ref, HYENA_SHORT))
        c = jnp.concatenate([depthwise(glu_ext, slab, hc, dww_ref, dwb_ref, CONF_KERNEL)
                             for slab in range(D_CONV // LANES)], axis=1)
        mu = jnp.mean(c, axis=-1, keepdims=True)
        cc = c - mu
        c = cc * lax.rsqrt(jnp.mean(cc * cc, axis=-1, keepdims=True) + EPS) * lng_ref[...] + lnb_ref[...]
        cact_ref[out_rows, :] = _silu(c).astype(BF16)

    x1 = _swiglu_half_step(x_ref[...], g_pre_ref[...], wg_ref, wu_ref, wd_ref, g_post_ref[...])
    x1_ref[...] = x1
    u = _rms(x1, g_mix_ref[...]).astype(BF16)
    proj = _dot(u, win_ref[...])
    o1 = 3 * D_HYENA
    hy = proj[:, :o1]
    glu = proj[:, o1:o1 + D_CONV] * _sigmoid(proj[:, o1 + D_CONV:])

    for ext, new, cur, h in ((hy_ext, hy_new, hy, hs), (glu_ext, glu_new, glu, hc)):
        for slab in range(cur.shape[1] // LANES):
            tail = ext[slab, rows(tm, h), :]
            ext[slab, rows(0, h), :] = jnp.where(new_starts_seq, 0.0, tail)
            ext[slab, rows(h, tm), :] = new[:, lanes(slab)]
            ext[slab, rows(h + tm, h), :] = jnp.where(cur_starts_seq, 0.0, cur[0:h, lanes(slab)])
        new[...] = cur


def _ffn_proj(x, g_pre, wg, wu, wd, g_post, g_mix, win_a, short_w, short_b, dw_w, dw_b, ln_g, ln_b):
    tokens = x.shape[0]
    tm = TOKEN_TILE
    n = tokens // tm
    cur = lambda width: pl.BlockSpec((tm, width), lambda s: (jnp.minimum(s, n - 1), 0))
    lag = lambda width: pl.BlockSpec((tm, width), lambda s: (jnp.maximum(s - 2, 0), 0))
    n_a = win_a.shape[1]
    return pl.pallas_call(
        _ffn_proj_kernel,
        grid=(n + 2,),
        in_specs=[cur(D_MODEL), _whole((1, D_MODEL)), _whole((D_MODEL, D_FF)),
                  _whole((D_MODEL, D_FF)), _whole((D_FF, D_MODEL)), _whole((1, D_MODEL)),
                  _whole((1, D_MODEL)), _whole((D_MODEL, n_a)),
                  _whole((HYENA_SHORT, 3 * D_HYENA)), _whole((1, 3 * D_HYENA)),
                  _whole((CONF_KERNEL, D_CONV)), _whole((1, D_CONV)), _whole((1, D_CONV)),
                  _whole((1, D_CONV))],
        out_specs=[cur(D_MODEL), lag(D_HYENA), lag(D_HYENA), lag(D_CONV)],
        out_shape=[jax.ShapeDtypeStruct((tokens, D_MODEL), F32),
                   jax.ShapeDtypeStruct((tokens, D_HYENA), F32),
                   jax.ShapeDtypeStruct((tokens, D_HYENA), F32),
                   jax.ShapeDtypeStruct((tokens, D_CONV), BF16)],
        scratch_shapes=[pltpu.VMEM((3 * D_HYENA // LANES, ROW_STEP * (tm + 2 * SHORT_HALO), LANES), F32),
                        pltpu.VMEM((D_CONV // LANES, ROW_STEP * (tm + 2 * CONV_HALO), LANES), F32),
                        pltpu.VMEM((tm, 3 * D_HYENA), F32),
                        pltpu.VMEM((tm, D_CONV), F32)],
        compiler_params=pltpu.CompilerParams(
            dimension_semantics=("arbitrary",), vmem_limit_bytes=VMEM_LIMIT_BYTES),
        name="ffn_proj",
    )(x, g_pre, wg, wu, wd, g_post, g_mix, win_a, short_w, short_b, dw_w, dw_b, ln_g, ln_b)


def _long_conv_kernel(sig_ref, hb_ref, fwd_ref, inv_ref, f2f_ref, f2i_ref, kr_ref, ki_ref,
                      z_ref, s_ref, r_ref, w_ref):
    def in_body(t1, carry):
        w_ref[_signal_rows(t1), :] = sig_ref[0, pl.ds(pl.multiple_of(t1 * N2, N2), N2), :]
        return carry
    lax.fori_loop(0, T1, in_body, 0, unroll=4)
    _forward_spectrum_rows(_time_major_rows(w_ref), s_ref, r_ref, fwd_ref)

    def freq_body(g, carry):
        x = _dot(f2f_ref[...], r_ref[g])
        xr, xi = x[0:N2], x[N2:2 * N2]
        kr, ki = kr_ref[0, g], ki_ref[0, g]
        y = jnp.concatenate([xr * kr - xi * ki, xr * ki + xi * kr], axis=0).astype(BF16)
        b = _dot(f2i_ref[...], y)
        for k in range(F1_GROUP):
            row0 = pl.multiple_of((g * F1_GROUP + k) * PITCH_B, SUBLANES)
            s_ref[pl.ds(row0, 2 * N2), :] = b[:, k * CHUNK:(k + 1) * CHUNK]
        return carry
    lax.fori_loop(0, N_GROUPS, freq_body, 0, unroll=True)

    def inv_body(t2, carry):
        g = jnp.concatenate([s_ref[pl.ds(t2, F1P, stride=PITCH_B), :],
                             s_ref[pl.ds(N2 + t2, F1P, stride=PITCH_B), :]], axis=0)
        w_ref[pl.ds(t2, T1, stride=PITCH_Y), :] = _dot(inv_ref[t2], g.astype(BF16))
        return carry
    lax.fori_loop(0, N2, inv_body, 0, unroll=T2_UNROLL)

    def out_body(t1, carry):
        rows = pl.ds(pl.multiple_of(t1 * N2, N2), N2)
        z_ref[0, rows, :] = w_ref[_signal_rows(t1), :] + sig_ref[0, rows, :] * hb_ref[...]
        return carry
    lax.fori_loop(0, T1, out_body, 0, unroll=4)


def _long_conv(sig, hy_bias, kr, ki):
    batch = sig.shape[0]
    fwd, inv, f2f, f2i = _fft_tables()
    spec_k = pl.BlockSpec((1, N_GROUPS, N2, GROUP_LANES), lambda c, b: (c, 0, 0, 0),
                          pipeline_mode=pl.Buffered(1))
    spec_io = pl.BlockSpec((1, SEQ, CHUNK), lambda c, b: (b, 0, c))
    return pl.pallas_call(
        _long_conv_kernel,
        grid=(N_CHUNKS, batch),
        in_specs=[spec_io, pl.BlockSpec((1, CHUNK), lambda c, b: (0, c)),
                  _whole((N2, 2 * F1P, T1)), _whole((N2, T1, 2 * F1P)),
                  _whole((2 * N2, 2 * N2)), _whole((2 * N2, 2 * N2)), spec_k, spec_k],
        out_specs=spec_io,
        out_shape=jax.ShapeDtypeStruct((batch, SEQ, D_HYENA), F32),
        scratch_shapes=[
            pltpu.VMEM((S_ROWS, CHUNK), F32),
            pltpu.VMEM((N_GROUPS, 2 * N2, GROUP_LANES), BF16),
            pltpu.VMEM((T1 * PITCH_Y, CHUNK), F32),
        ],
        compiler_params=pltpu.CompilerParams(
            dimension_semantics=("arbitrary", "arbitrary"), vmem_limit_bytes=VMEM_LIMIT_BYTES),
        name="long_conv",
    )(sig, hy_bias.reshape(1, -1),
      jnp.asarray(fwd).astype(BF16), jnp.asarray(inv).astype(BF16),
      jnp.asarray(f2f).astype(BF16), jnp.asarray(f2i).astype(BF16), kr, ki)


def _mix_ffn_kernel(x1_ref, z_ref, x0_ref, cact_ref, g_mix_ref, wgate_ref, hyw_ref, cvw_ref,
                    wout_ref, g_mixpost_ref, g_pre_ref, wg_ref, wu_ref, wd_ref, g_post_ref,
                    out_ref):
    y_a = _dot((z_ref[...] * x0_ref[...]).astype(BF16), hyw_ref[...])
    y_b = _dot(cact_ref[...], cvw_ref[...])
    x1 = x1_ref[...]
    u = _rms(x1, g_mix_ref[...]).astype(BF16)
    gates = _dot(u, wgate_ref[...])
    merged = _sigmoid(gates[:, :D_MODEL]) * y_a + _sigmoid(gates[:, D_MODEL:]) * y_b
    m = _dot(merged.astype(BF16), wout_ref[...])
    x2 = x1 + _rms(m, g_mixpost_ref[...])
    out_ref[...] = _swiglu_half_step(x2, g_pre_ref[...], wg_ref, wu_ref, wd_ref, g_post_ref[...])


def _mix_ffn(x1, z, x0, cact, g_mix, w_gates, hy_w_out, cv_w_out, w_out, g_mixpost,
             g_pre, wg, wu, wd, g_post):
    tokens = x1.shape[0]
    tm = TOKEN_TILE
    tile = lambda width: pl.BlockSpec((tm, width), lambda i: (i, 0))
    return pl.pallas_call(
        _mix_ffn_kernel,
        grid=(tokens // tm,),
        in_specs=[tile(D_MODEL), tile(D_HYENA), tile(D_HYENA), tile(D_CONV),
                  _whole((1, D_MODEL)), _whole((D_MODEL, 2 * D_MODEL)),
                  _whole((D_HYENA, D_MODEL)), _whole((D_CONV, D_MODEL)), _whole((D_MODEL, D_MODEL)),
                  _whole((1, D_MODEL)), _whole((1, D_MODEL)), _whole((D_MODEL, D_FF)),
                  _whole((D_MODEL, D_FF)), _whole((D_FF, D_MODEL)), _whole((1, D_MODEL))],
        out_specs=tile(D_MODEL),
        out_shape=jax.ShapeDtypeStruct((tokens, D_MODEL), F32),
        compiler_params=pltpu.CompilerParams(
            dimension_semantics=("arbitrary",), vmem_limit_bytes=VMEM_LIMIT_BYTES),
        name="mix_ffn",
    )(x1, z, x0, cact, g_mix, w_gates, hy_w_out, cv_w_out, w_out, g_mixpost, g_pre, wg, wu, wd, g_post)


def _encoder_layer(xs, p):
    row = lambda v: v.reshape(1, -1)
    bf = lambda w: w.astype(BF16)
    o2 = 3 * D_HYENA + 2 * D_CONV
    kr, ki = _filter_spectrum(p["hy_filt_w1"], p["hy_filt_b1"], p["hy_filt_freq1"], p["hy_filt_w2"],
                              p["hy_filt_b2"], p["hy_filt_freq2"], p["hy_filt_w3"])
    win_a = bf(p["w_in"][:, :o2])
    w_gates = bf(p["w_in"][:, o2:])
    ffn1 = (row(p["ffn1_norm_pre"]), bf(p["ffn1_w_gate"]), bf(p["ffn1_w_up"]), bf(p["ffn1_w_down"]),
            row(p["ffn1_norm_post"]))
    ffn2 = (row(p["ffn2_norm_pre"]), bf(p["ffn2_w_gate"]), bf(p["ffn2_w_up"]), bf(p["ffn2_w_down"]),
            row(p["ffn2_norm_post"]))
    g_mix = row(p["mix_norm_pre"])
    outs = []
    for x in xs:
        batch, seq, _ = x.shape
        assert seq == SEQ and (batch * seq) % TOKEN_TILE == 0
        x1, sig, x0, cact = _ffn_proj(
            x.reshape(batch * seq, D_MODEL), *ffn1, g_mix, win_a, p["hy_short_w"],
            row(p["hy_short_b"]), p["cv_dw_w"], row(p["cv_dw_b"]), row(p["cv_ln_g"]), row(p["cv_ln_b"]))
        z = _long_conv(sig.reshape(batch, seq, D_HYENA), p["hy_bias"], kr, ki)
        y = _mix_ffn(x1, z.reshape(batch * seq, D_HYENA), x0, cact, g_mix, w_gates,
                     bf(p["hy_w_out"]), bf(p["cv_w_out"]), bf(p["w_out"]), row(p["mix_norm_post"]), *ffn2)
        outs.append(y.reshape(batch, seq, D_MODEL))
    return outs


_PARAM_NAMES = (
    "ffn1_norm_pre", "ffn1_w_gate", "ffn1_w_up", "ffn1_w_down", "ffn1_norm_post",
    "mix_norm_pre", "w_in", "hy_short_w", "hy_short_b",
    "hy_filt_w1", "hy_filt_b1", "hy_filt_freq1", "hy_filt_w2", "hy_filt_b2", "hy_filt_freq2",
    "hy_filt_w3", "hy_bias", "hy_w_out", "cv_dw_w", "cv_dw_b", "cv_ln_g", "cv_ln_b", "cv_w_out",
    "w_out", "mix_norm_post",
    "ffn2_norm_pre", "ffn2_w_gate", "ffn2_w_up", "ffn2_w_down", "ffn2_norm_post")


def kernel(x_prompt, x_sample, ffn1_norm_pre, ffn1_w_gate, ffn1_w_up, ffn1_w_down, ffn1_norm_post, mix_norm_pre, w_in, hy_short_w, hy_short_b, hy_filt_w1, hy_filt_b1, hy_filt_freq1, hy_filt_w2, hy_filt_b2, hy_filt_freq2, hy_filt_w3, hy_bias, hy_w_out, cv_dw_w, cv_dw_b, cv_ln_g, cv_ln_b, cv_w_out, w_out, mix_norm_post, ffn2_norm_pre, ffn2_w_gate, ffn2_w_up, ffn2_w_down, ffn2_norm_post):
    stacked = dict(zip(_PARAM_NAMES, (
        ffn1_norm_pre, ffn1_w_gate, ffn1_w_up, ffn1_w_down, ffn1_norm_post,
        mix_norm_pre, w_in, hy_short_w, hy_short_b,
        hy_filt_w1, hy_filt_b1, hy_filt_freq1, hy_filt_w2, hy_filt_b2, hy_filt_freq2,
        hy_filt_w3, hy_bias, hy_w_out, cv_dw_w, cv_dw_b, cv_ln_g, cv_ln_b, cv_w_out,
        w_out, mix_norm_post,
        ffn2_norm_pre, ffn2_w_gate, ffn2_w_up, ffn2_w_down, ffn2_norm_post)))
    xs = [x_prompt, x_sample]
    for layer in range(ffn1_norm_pre.shape[0]):
        xs = _encoder_layer(xs, {k: v[layer] for k, v in stacked.items()})
    return (xs[0], xs[1])
```

```python
import functools
import math

import jax
import jax.numpy as jnp
import numpy as np
from jax import lax
from jax.experimental import pallas as pl
from jax.experimental.pallas import tpu as pltpu

D_MODEL = 1024
D_HYENA = 512
D_CONV = 512
D_FF = 2816
HYENA_SHORT = 3
CONF_KERNEL = 31
FILTER_EMB = 33
FILTER_BANDS = (FILTER_EMB - 1) // 2
FILTER_HIDDEN = 64
DECAY_TARGET = 1e-2
FAST_DECAY_PCT = 0.3
SLOW_DECAY_PCT = 1.5
EPS = 1e-6

F32 = jnp.float32
BF16 = jnp.bfloat16

LANES = 128
SUBLANES = 8
VMEM_LIMIT_BYTES = 60 * 1024 * 1024

SEQ = 8192
NFFT = 2 * SEQ
N2 = 128
N1 = NFFT // N2
T1 = SEQ // N2
F1P = 72
CHUNK = LANES
N_CHUNKS = D_HYENA // CHUNK
F1_GROUP = 8
N_GROUPS = F1P // F1_GROUP
GROUP_LANES = F1_GROUP * CHUNK
PITCH_A = 2 * F1P + SUBLANES
PITCH_B = 2 * N2 + SUBLANES
PITCH_Y = N2 + SUBLANES
T2_UNROLL = 32
FILTER_COLS = 1024

TOKEN_TILE = 512
CONV_ROWS = 32
CONV_HALO = 16
SHORT_HALO = 8
ROW_STEP = 2


@functools.lru_cache(maxsize=None)
def _fft_tables():
    t1 = np.arange(T1, dtype=np.int64)
    t2 = np.arange(N2, dtype=np.int64)
    f1 = np.arange(F1P, dtype=np.int64)
    keep = (f1 <= N1 // 2).astype(np.float64)
    k = (f1[None, :, None] * (N2 * t1[None, None, :] + t2[:, None, None])) % NFFT
    ph = 2.0 * np.pi * k.astype(np.float64) / NFFT
    fwd = np.concatenate([np.cos(ph) * keep[None, :, None],
                          -np.sin(ph) * keep[None, :, None]], axis=1)
    wgt = np.where((f1 == 0) | (f1 == N1 // 2), 1.0, 2.0) * keep / NFFT
    phq = np.transpose(ph, (0, 2, 1))
    inv = np.concatenate([np.cos(phq) * wgt[None, None, :],
                          -np.sin(phq) * wgt[None, None, :]], axis=2)
    a = np.arange(N2, dtype=np.int64)
    th = 2.0 * np.pi * ((a[:, None] * a[None, :]) % N2).astype(np.float64) / N2
    c, s = np.cos(th), np.sin(th)
    f2_fwd = np.block([[c, s], [-s, c]])
    f2_inv = np.block([[c, -s], [s, c]])
    pair = lambda m: np.concatenate([m[0::2], m[1::2]], axis=2)
    f32 = np.float32
    return pair(fwd).astype(f32), pair(inv).astype(f32), f2_fwd.astype(f32), f2_inv.astype(f32)


def _filter_constants():
    bands = np.linspace(1e-4, FILTER_BANDS - 1, FILTER_BANDS).astype(np.float32)
    max_decay = math.log(DECAY_TARGET) / FAST_DECAY_PCT
    min_decay = math.log(DECAY_TARGET) / SLOW_DECAY_PCT
    deltas = np.abs(np.linspace(min_decay, max_decay, D_HYENA)).astype(np.float32)
    return bands.reshape(FILTER_BANDS, 1), deltas.reshape(1, D_HYENA)


def _whole(shape):
    zeros = (0,) * len(shape)
    return pl.BlockSpec(shape, lambda *_: zeros, pipeline_mode=pl.Buffered(1))


def _rms(x, g):
    return x * lax.rsqrt(jnp.mean(x * x, axis=-1, keepdims=True) + EPS) * g


def _dot(a, b):
    return jnp.dot(a, b, preferred_element_type=F32)


def _sigmoid(x):
    return 0.5 * jnp.tanh(0.5 * x) + 0.5


def _silu(x):
    half = 0.5 * x
    return half * jnp.tanh(half) + half


def _swiglu_half_step(x, g_pre, wg_ref, wu_ref, wd_ref, g_post):
    h = _rms(x, g_pre).astype(BF16)
    gate = _dot(h, wg_ref[...])
    up = _dot(h, wu_ref[...])
    act = (_silu(gate) * up).astype(BF16)
    return x + 0.5 * _rms(_dot(act, wd_ref[...]), g_post)


def _block_diag(a, b):
    zero = jnp.zeros_like(a)
    return jnp.concatenate([jnp.concatenate([a, zero], axis=1),
                            jnp.concatenate([zero, b], axis=1)], axis=0)


def _t1_dft(load_signal, s_ref, fwd_ref):
    def body(p, carry):
        t2 = 2 * p
        sig = _block_diag(load_signal(t2).astype(BF16), load_signal(t2 + 1).astype(BF16))
        a = _dot(fwd_ref[p], sig)
        row0 = pl.multiple_of(t2 * PITCH_A, SUBLANES)
        s_ref[pl.ds(row0, 2 * F1P), :] = a[:, :CHUNK]
        s_ref[pl.ds(row0 + PITCH_A, 2 * F1P), :] = a[:, CHUNK:]
        return carry
    lax.fori_loop(0, N2 // 2, body, 0, unroll=T2_UNROLL // 2)


def _t2_major_group(s_ref, g):
    cols = []
    for k in range(F1_GROUP):
        f1 = g * F1_GROUP + k
        cols.append(jnp.concatenate([s_ref[pl.ds(f1, N2, stride=PITCH_A), :],
                                     s_ref[pl.ds(F1P + f1, N2, stride=PITCH_A), :]], axis=0).astype(BF16))
    return jnp.concatenate(cols, axis=1)


def _gather_t2_major(s_ref, r_ref):
    def body(g, carry):
        r_ref[g] = _t2_major_group(s_ref, g)
        return carry
    lax.fori_loop(0, N_GROUPS, body, 0)


def _forward_spectrum_rows(load_signal, s_ref, r_ref, fwd_ref):
    _t1_dft(load_signal, s_ref, fwd_ref)
    _gather_t2_major(s_ref, r_ref)


def _time_major_rows(w_ref):
    return lambda t2: w_ref[pl.ds(t2, T1, stride=PITCH_Y), :]


def _signal_rows(t1):
    return pl.ds(pl.multiple_of(t1 * PITCH_Y, SUBLANES), N2)


def _filter_kernel(w1t_ref, w1c_ref, w1s_ref, b1_ref, fr1_ref, w2_ref, b2_ref, fr2_ref,
                   w3f_ref, w3b_ref, bands_ref, delta_ref, fwd_ref, f2f_ref,
                   kr_ref, ki_ref, h_ref, w_ref, s_ref, rf_ref, rb_ref):
    hi = lax.Precision.HIGHEST

    def dot_hi(a, b):
        return jnp.dot(a, b, preferred_element_type=F32, precision=hi)

    @pl.when(pl.program_id(0) == 0)
    def _():
        def body(i, carry):
            col0 = pl.multiple_of(i * FILTER_COLS, FILTER_COLS)
            n = (col0 + lax.broadcasted_iota(jnp.int32, (1, FILTER_COLS), 1)).astype(F32)
            t = n * (1.0 / (SEQ - 1))
            ang = bands_ref[...] * (n * (2.0 * math.pi / SEQ))
            zw = (w1t_ref[...] * t + dot_hi(w1c_ref[...], jnp.cos(ang))
                  - dot_hi(w1s_ref[...], jnp.sin(ang)))
            h = jnp.sin(fr1_ref[...] * (zw + b1_ref[...]))
            h = jnp.sin(fr2_ref[...] * (dot_hi(w2_ref[...], h) + b2_ref[...]))
            for k in range(FILTER_COLS // N2):
                h_ref[i * (FILTER_COLS // N2) + k] = h[:, k * N2:(k + 1) * N2]
            return carry
        lax.fori_loop(0, SEQ // FILTER_COLS, body, 0)

    def fill(w3_ref):
        def body(t1, ss):
            row0 = pl.multiple_of(t1 * N2, N2)
            n = (row0 + lax.broadcasted_iota(jnp.int32, (N2, 1), 0)).astype(F32)
            decay = jnp.exp(-(n * (1.0 / (SEQ - 1))) * delta_ref[...])
            rows = dot_hi(h_ref[t1].T, w3_ref[...]) * decay
            w_ref[_signal_rows(t1), :] = rows
            return ss + jnp.sum(rows * rows, axis=0, keepdims=True)
        return lax.fori_loop(0, T1, body, jnp.zeros((1, CHUNK), F32), unroll=4)

    ss_f = fill(w3f_ref)
    f0 = w_ref[0:1, :]
    _forward_spectrum_rows(_time_major_rows(w_ref), s_ref, rf_ref, fwd_ref)
    ss_b = fill(w3b_ref)
    b0 = w_ref[0:1, :]
    _forward_spectrum_rows(_time_major_rows(w_ref), s_ref, rb_ref, fwd_ref)
    scale = lax.rsqrt(ss_f + ss_b + 2.0 * f0 * b0 + EPS)
    scale = jnp.concatenate([scale] * F1_GROUP, axis=1)

    def body(g, carry):
        xf = _dot(f2f_ref[...], rf_ref[g])
        xb = _dot(f2f_ref[...], rb_ref[g])
        kr_ref[0, g] = scale * (xf[0:N2] + xb[0:N2])
        ki_ref[0, g] = scale * (xf[N2:2 * N2] - xb[N2:2 * N2])
        return carry
    lax.fori_loop(0, N_GROUPS, body, 0)


def _filter_spectrum(w1, b1, fr1, w2, b2, fr2, w3):
    fwd, _, f2f, _ = _fft_tables()
    bands, deltas = _filter_constants()
    col = lambda v: v.reshape(-1, 1)
    small = lambda shape: pl.BlockSpec(shape, lambda c: (0,) * len(shape))
    spec_out = pl.BlockSpec((1, N_GROUPS, N2, GROUP_LANES), lambda c: (c, 0, 0, 0))
    out_shape = jax.ShapeDtypeStruct((N_CHUNKS, N_GROUPS, N2, GROUP_LANES), F32)
    hid = FILTER_HIDDEN
    return pl.pallas_call(
        _filter_kernel,
        grid=(N_CHUNKS,),
        in_specs=[
            small((hid, 1)), small((hid, FILTER_BANDS)), small((hid, FILTER_BANDS)),
            small((hid, 1)), small((hid, 1)), small((hid, hid)), small((hid, 1)), small((hid, 1)),
            pl.BlockSpec((hid, CHUNK), lambda c: (0, c)),
            pl.BlockSpec((hid, CHUNK), lambda c: (0, N_CHUNKS + c)),
            small((FILTER_BANDS, 1)),
            pl.BlockSpec((1, CHUNK), lambda c: (0, c)),
            _whole((N2 // 2, 2 * F1P, 2 * T1)), _whole((2 * N2, 2 * N2)),
        ],
        out_specs=[spec_out, spec_out],
        out_shape=[out_shape, out_shape],
        scratch_shapes=[
            pltpu.VMEM((T1, hid, N2), F32),
            pltpu.VMEM((T1 * PITCH_Y, CHUNK), F32),
            pltpu.VMEM((N2 * PITCH_A, CHUNK), F32),
            pltpu.VMEM((N_GROUPS, 2 * N2, GROUP_LANES), BF16),
            pltpu.VMEM((N_GROUPS, 2 * N2, GROUP_LANES), BF16),
        ],
        compiler_params=pltpu.CompilerParams(
            dimension_semantics=("arbitrary",), vmem_limit_bytes=VMEM_LIMIT_BYTES),
        name="filter_spectrum",
    )(col(w1[0]), w1[1:1 + FILTER_BANDS].T, w1[1 + FILTER_BANDS:].T, col(b1), col(fr1), w2.T,
      col(b2), col(fr2), w3, w3, jnp.asarray(bands), jnp.asarray(deltas),
      jnp.asarray(fwd).astype(BF16), jnp.asarray(f2f).astype(BF16))


def _ffn_proj_kernel(x_ref, g_pre_ref, wg_ref, wu_ref, wd_ref, g_post_ref, g_mix_ref, win_ref,
                     sw_ref, sb_ref, dww_ref, dwb_ref, lng_ref, lnb_ref,
                     x1_ref, sig_ref, x0_ref, cact_ref, hy_ext, glu_ext, hy_new, glu_new):
    tm = TOKEN_TILE
    hs, hc = SHORT_HALO, CONV_HALO
    step = pl.program_id(0)
    tiles_per_seq = SEQ // tm
    new_starts_seq = (step + tiles_per_seq - 1) % tiles_per_seq == 0
    cur_starts_seq = step % tiles_per_seq == 0

    def rows(start, size):
        return pl.ds(ROW_STEP * start, size, stride=ROW_STEP)

    def lanes(slab):
        return slice(slab * LANES, (slab + 1) * LANES)

    @pl.when(step == 0)
    def _():
        hy_ext[...] = jnp.zeros_like(hy_ext)
        glu_ext[...] = jnp.zeros_like(glu_ext)
        hy_new[...] = jnp.zeros_like(hy_new)
        glu_new[...] = jnp.zeros_like(glu_new)

    n_h = D_HYENA // LANES
    for r0 in range(0, tm, CONV_ROWS):
        def depthwise(ext, slab, halo, w_ref, b_ref, taps):
            acc = b_ref[:, lanes(slab)]
            for j in range(taps):
                acc = acc + (w_ref[j:j + 1, lanes(slab)]
                             * ext[slab, rows(r0 + halo + j - taps // 2, CONV_ROWS), :])
            return acc

        out_rows = slice(r0, r0 + CONV_ROWS)
        for slab in range(n_h):
            x0_ref[out_rows, lanes(slab)] = depthwise(hy_ext, slab, hs, sw_ref, sb_ref, HYENA_SHORT)
            sig_ref[out_rows, lanes(slab)] = (
                depthwise(hy_ext, 2 * n_h + slab, hs, sw_ref, sb_ref, HYENA_SHORT)
                * depthwise(hy_ext, n_h + slab, hs, sw_ref, sb_ref, HYENA_SHORT))
        c = jnp.concatenate([depthwise(glu_ext, slab, hc, dww_ref, dwb_ref, CONF_KERNEL)
                             for slab in range(D_CONV // LANES)], axis=1)
        mu = jnp.mean(c, axis=-1, keepdims=True)
        cc = c - mu
        c = cc * lax.rsqrt(jnp.mean(cc * cc, axis=-1, keepdims=True) + EPS) * lng_ref[...] + lnb_ref[...]
        cact_ref[out_rows, :] = _silu(c).astype(BF16)

    x1 = _swiglu_half_step(x_ref[...], g_pre_ref[...], wg_ref, wu_ref, wd_ref, g_post_ref[...])
    x1_ref[...] = x1
    u = _rms(x1, g_mix_ref[...]).astype(BF16)
    proj = _dot(u, win_ref[...])
    o1 = 3 * D_HYENA
    hy = proj[:, :o1]
    glu = proj[:, o1:o1 + D_CONV] * _sigmoid(proj[:, o1 + D_CONV:])

    for ext, new, cur, h in ((hy_ext, hy_new, hy, hs), (glu_ext, glu_new, glu, hc)):
        for slab in range(cur.shape[1] // LANES):
            tail = ext[slab, rows(tm, h), :]
            ext[slab, rows(0, h), :] = jnp.where(new_starts_seq, 0.0, tail)
            ext[slab, rows(h, tm), :] = new[:, lanes(slab)]
            ext[slab, rows(h + tm, h), :] = jnp.where(cur_starts_seq, 0.0, cur[0:h, lanes(slab)])
        new[...] = cur


def _ffn_proj(x, g_pre, wg, wu, wd, g_post, g_mix, win_a, short_w, short_b, dw_w, dw_b, ln_g, ln_b):
    tokens = x.shape[0]
    tm = TOKEN_TILE
    n = tokens // tm
    cur = lambda width: pl.BlockSpec((tm, width), lambda s: (jnp.minimum(s, n - 1), 0))
    lag = lambda width: pl.BlockSpec((tm, width), lambda s: (jnp.maximum(s - 2, 0), 0))
    n_a = win_a.shape[1]
    return pl.pallas_call(
        _ffn_proj_kernel,
        grid=(n + 2,),
        in_specs=[cur(D_MODEL), _whole((1, D_MODEL)), _whole((D_MODEL, D_FF)),
                  _whole((D_MODEL, D_FF)), _whole((D_FF, D_MODEL)), _whole((1, D_MODEL)),
                  _whole((1, D_MODEL)), _whole((D_MODEL, n_a)),
                  _whole((HYENA_SHORT, 3 * D_HYENA)), _whole((1, 3 * D_HYENA)),
                  _whole((CONF_KERNEL, D_CONV)), _whole((1, D_CONV)), _whole((1, D_CONV)),
                  _whole((1, D_CONV))],
        out_specs=[cur(D_MODEL), lag(D_HYENA), lag(D_HYENA), lag(D_CONV)],
        out_shape=[jax.ShapeDtypeStruct((tokens, D_MODEL), F32),
                   jax.ShapeDtypeStruct((tokens, D_HYENA), F32),
                   jax.ShapeDtypeStruct((tokens, D_HYENA), F32),
                   jax.ShapeDtypeStruct((tokens, D_CONV), BF16)],
        scratch_shapes=[pltpu.VMEM((3 * D_HYENA // LANES, ROW_STEP * (tm + 2 * SHORT_HALO), LANES), F32),
                        pltpu.VMEM((D_CONV // LANES, ROW_STEP * (tm + 2 * CONV_HALO), LANES), F32),
                        pltpu.VMEM((tm, 3 * D_HYENA), F32),
                        pltpu.VMEM((tm, D_CONV), F32)],
        compiler_params=pltpu.CompilerParams(
            dimension_semantics=("arbitrary",), vmem_limit_bytes=VMEM_LIMIT_BYTES),
        name="ffn_proj",
    )(x, g_pre, wg, wu, wd, g_post, g_mix, win_a, short_w, short_b, dw_w, dw_b, ln_g, ln_b)


def _long_conv_kernel(sig_ref, hb_ref, fwd_ref, inv_ref, f2f_ref, f2i_ref, kr_ref, ki_ref,
                      z_ref, s_ref, b_ref, w_ref):
    def in_body(t1, carry):
        w_ref[_signal_rows(t1), :] = sig_ref[0, pl.ds(pl.multiple_of(t1 * N2, N2), N2), :]
        return carry
    lax.fori_loop(0, T1, in_body, 0, unroll=4)
    _t1_dft(_time_major_rows(w_ref), s_ref, fwd_ref)

    for g in range(N_GROUPS):
        x = _dot(f2f_ref[...], _t2_major_group(s_ref, g))
        xr, xi = x[0:N2], x[N2:2 * N2]
        kr, ki = kr_ref[0, g], ki_ref[0, g]
        y = jnp.concatenate([xr * kr - xi * ki, xr * ki + xi * kr], axis=0).astype(BF16)
        b = _dot(f2i_ref[...], y)
        for k in range(F1_GROUP):
            row0 = (g * F1_GROUP + k) * PITCH_B
            b_ref[row0:row0 + 2 * N2, :] = b[:, k * CHUNK:(k + 1) * CHUNK]

    def spectrum_rows(t2):
        return jnp.concatenate([b_ref[pl.ds(t2, F1P, stride=PITCH_B), :],
                                b_ref[pl.ds(N2 + t2, F1P, stride=PITCH_B), :]], axis=0).astype(BF16)

    def inv_body(p, carry):
        t2 = 2 * p
        y = _dot(inv_ref[p], _block_diag(spectrum_rows(t2), spectrum_rows(t2 + 1)))
        w_ref[pl.ds(t2, T1, stride=PITCH_Y), :] = y[:, :CHUNK]
        w_ref[pl.ds(t2 + 1, T1, stride=PITCH_Y), :] = y[:, CHUNK:]
        return carry
    lax.fori_loop(0, N2 // 2, inv_body, 0, unroll=T2_UNROLL // 2)

    def out_body(t1, carry):
        rows = pl.ds(pl.multiple_of(t1 * N2, N2), N2)
        z_ref[0, rows, :] = w_ref[_signal_rows(t1), :] + sig_ref[0, rows, :] * hb_ref[...]
        return carry
    lax.fori_loop(0, T1, out_body, 0, unroll=4)


def _long_conv(sig, hy_bias, kr, ki):
    batch = sig.shape[0]
    fwd, inv, f2f, f2i = _fft_tables()
    spec_k = pl.BlockSpec((1, N_GROUPS, N2, GROUP_LANES), lambda c, b: (c, 0, 0, 0),
                          pipeline_mode=pl.Buffered(1))
    spec_io = pl.BlockSpec((1, SEQ, CHUNK), lambda c, b: (b, 0, c))
    return pl.pallas_call(
        _long_conv_kernel,
        grid=(N_CHUNKS, batch),
        in_specs=[spec_io, pl.BlockSpec((1, CHUNK), lambda c, b: (0, c)),
                  _whole((N2 // 2, 2 * F1P, 2 * T1)), _whole((N2 // 2, T1, 4 * F1P)),
                  _whole((2 * N2, 2 * N2)), _whole((2 * N2, 2 * N2)), spec_k, spec_k],
        out_specs=spec_io,
        out_shape=jax.ShapeDtypeStruct((batch, SEQ, D_HYENA), F32),
        scratch_shapes=[
            pltpu.VMEM((N2 * PITCH_A, CHUNK), F32),
            pltpu.VMEM((F1P * PITCH_B, CHUNK), F32),
            pltpu.VMEM((T1 * PITCH_Y, CHUNK), F32),
        ],
        compiler_params=pltpu.CompilerParams(
            dimension_semantics=("arbitrary", "arbitrary"), vmem_limit_bytes=VMEM_LIMIT_BYTES),
        name="long_conv",
    )(sig, hy_bias.reshape(1, -1),
      jnp.asarray(fwd).astype(BF16), jnp.asarray(inv).astype(BF16),
      jnp.asarray(f2f).astype(BF16), jnp.asarray(f2i).astype(BF16), kr, ki)


def _mix_ffn_kernel(x1_ref, z_ref, x0_ref, cact_ref, g_mix_ref, wgate_ref, hyw_ref, cvw_ref,
                    wout_ref, g_mixpost_ref, g_pre_ref, wg_ref, wu_ref, wd_ref, g_post_ref,
                    out_ref):
    y_a = _dot((z_ref[...] * x0_ref[...]).astype(BF16), hyw_ref[...])
    y_b = _dot(cact_ref[...], cvw_ref[...])
    x1 = x1_ref[...]
    u = _rms(x1, g_mix_ref[...]).astype(BF16)
    gates = _dot(u, wgate_ref[...])
    merged = _sigmoid(gates[:, :D_MODEL]) * y_a + _sigmoid(gates[:, D_MODEL:]) * y_b
    m = _dot(merged.astype(BF16), wout_ref[...])
    x2 = x1 + _rms(m, g_mixpost_ref[...])
    out_ref[...] = _swiglu_half_step(x2, g_pre_ref[...], wg_ref, wu_ref, wd_ref, g_post_ref[...])


def _mix_ffn(x1, z, x0, cact, g_mix, w_gates, hy_w_out, cv_w_out, w_out, g_mixpost,
             g_pre, wg, wu, wd, g_post):
    tokens = x1.shape[0]
    tm = TOKEN_TILE
    tile = lambda width: pl.BlockSpec((tm, width), lambda i: (i, 0))
    return pl.pallas_call(
        _mix_ffn_kernel,
        grid=(tokens // tm,),
        in_specs=[tile(D_MODEL), tile(D_HYENA), tile(D_HYENA), tile(D_CONV),
                  _whole((1, D_MODEL)), _whole((D_MODEL, 2 * D_MODEL)),
                  _whole((D_HYENA, D_MODEL)), _whole((D_CONV, D_MODEL)), _whole((D_MODEL, D_MODEL)),
                  _whole((1, D_MODEL)), _whole((1, D_MODEL)), _whole((D_MODEL, D_FF)),
                  _whole((D_MODEL, D_FF)), _whole((D_FF, D_MODEL)), _whole((1, D_MODEL))],
        out_specs=tile(D_MODEL),
        out_shape=jax.ShapeDtypeStruct((tokens, D_MODEL), F32),
        compiler_params=pltpu.CompilerParams(
            dimension_semantics=("arbitrary",), vmem_limit_bytes=VMEM_LIMIT_BYTES),
        name="mix_ffn",
    )(x1, z, x0, cact, g_mix, w_gates, hy_w_out, cv_w_out, w_out, g_mixpost, g_pre, wg, wu, wd, g_post)


def _encoder_layer(xs, p):
    row = lambda v: v.reshape(1, -1)
    bf = lambda w: w.astype(BF16)
    o2 = 3 * D_HYENA + 2 * D_CONV
    kr, ki = _filter_spectrum(p["hy_filt_w1"], p["hy_filt_b1"], p["hy_filt_freq1"], p["hy_filt_w2"],
                              p["hy_filt_b2"], p["hy_filt_freq2"], p["hy_filt_w3"])
    win_a = bf(p["w_in"][:, :o2])
    w_gates = bf(p["w_in"][:, o2:])
    ffn1 = (row(p["ffn1_norm_pre"]), bf(p["ffn1_w_gate"]), bf(p["ffn1_w_up"]), bf(p["ffn1_w_down"]),
            row(p["ffn1_norm_post"]))
    ffn2 = (row(p["ffn2_norm_pre"]), bf(p["ffn2_w_gate"]), bf(p["ffn2_w_up"]), bf(p["ffn2_w_down"]),
            row(p["ffn2_norm_post"]))
    g_mix = row(p["mix_norm_pre"])
    outs = []
    for x in xs:
        batch, seq, _ = x.shape
        assert seq == SEQ and (batch * seq) % TOKEN_TILE == 0
        x1, sig, x0, cact = _ffn_proj(
            x.reshape(batch * seq, D_MODEL), *ffn1, g_mix, win_a, p["hy_short_w"],
            row(p["hy_short_b"]), p["cv_dw_w"], row(p["cv_dw_b"]), row(p["cv_ln_g"]), row(p["cv_ln_b"]))
        z = _long_conv(sig.reshape(batch, seq, D_HYENA), p["hy_bias"], kr, ki)
        y = _mix_ffn(x1, z.reshape(batch * seq, D_HYENA), x0, cact, g_mix, w_gates,
                     bf(p["hy_w_out"]), bf(p["cv_w_out"]), bf(p["w_out"]), row(p["mix_norm_post"]), *ffn2)
        outs.append(y.reshape(batch, seq, D_MODEL))
    return outs


_PARAM_NAMES = (
    "ffn1_norm_pre", "ffn1_w_gate", "ffn1_w_up", "ffn1_w_down", "ffn1_norm_post",
    "mix_norm_pre", "w_in", "hy_short_w", "hy_short_b",
    "hy_filt_w1", "hy_filt_b1", "hy_filt_freq1", "hy_filt_w2", "hy_filt_b2", "hy_filt_freq2",
    "hy_filt_w3", "hy_bias", "hy_w_out", "cv_dw_w", "cv_dw_b", "cv_ln_g", "cv_ln_b", "cv_w_out",
    "w_out", "mix_norm_post",
    "ffn2_norm_pre", "ffn2_w_gate", "ffn2_w_up", "ffn2_w_down", "ffn2_norm_post")


def kernel(x_prompt, x_sample, ffn1_norm_pre, ffn1_w_gate, ffn1_w_up, ffn1_w_down, ffn1_norm_post, mix_norm_pre, w_in, hy_short_w, hy_short_b, hy_filt_w1, hy_filt_b1, hy_filt_freq1, hy_filt_w2, hy_filt_b2, hy_filt_freq2, hy_filt_w3, hy_bias, hy_w_out, cv_dw_w, cv_dw_b, cv_ln_g, cv_ln_b, cv_w_out, w_out, mix_norm_post, ffn2_norm_pre, ffn2_w_gate, ffn2_w_up, ffn2_w_down, ffn2_norm_post):
    stacked = dict(zip(_PARAM_NAMES, (
        ffn1_norm_pre, ffn1_w_gate, ffn1_w_up, ffn1_w_down, ffn1_norm_post,
        mix_norm_pre, w_in, hy_short_w, hy_short_b,
        hy_filt_w1, hy_filt_b1, hy_filt_freq1, hy_filt_w2, hy_filt_b2, hy_filt_freq2,
        hy_filt_w3, hy_bias, hy_w_out, cv_dw_w, cv_dw_b, cv_ln_g, cv_ln_b, cv_w_out,
        w_out, mix_norm_post,
        ffn2_norm_pre, ffn2_w_gate, ffn2_w_up, ffn2_w_down, ffn2_norm_post)))
    xs = [x_prompt, x_sample]
    for layer in range(ffn1_norm_pre.shape[0]):
        xs = _encoder_layer(xs, {k: v[layer] for k, v in stacked.items()})
    return (xs[0], xs[1])
```

```python
import functools
import math

import jax
import jax.numpy as jnp
import numpy as np
from jax import lax
from jax.experimental import pallas as pl
from jax.experimental.pallas import tpu as pltpu

D_MODEL = 1024
D_HYENA = 512
D_CONV = 512
D_FF = 2816
HYENA_SHORT = 3
CONF_KERNEL = 31
FILTER_EMB = 33
FILTER_BANDS = (FILTER_EMB - 1) // 2
FILTER_HIDDEN = 64
DECAY_TARGET = 1e-2
FAST_DECAY_PCT = 0.3
SLOW_DECAY_PCT = 1.5
EPS = 1e-6

F32 = jnp.float32
BF16 = jnp.bfloat16

LANES = 128
SUBLANES = 8
VMEM_LIMIT_BYTES = 60 * 1024 * 1024

SEQ = 8192
NFFT = 2 * SEQ
N2 = 128
N1 = NFFT // N2
T1 = SEQ // N2
F1P = 72
CHUNK = LANES
N_CHUNKS = D_HYENA // CHUNK
F1_GROUP = 8
N_GROUPS = F1P // F1_GROUP
GROUP_LANES = F1_GROUP * CHUNK
PITCH_A = 2 * F1P + SUBLANES
PITCH_B = 2 * N2 + SUBLANES
PITCH_Y = N2 + SUBLANES
T2_UNROLL = 64
FILTER_COLS = 1024

TOKEN_TILE = 512
CONV_ROWS = 32
CONV_HALO = 16
SHORT_HALO = 8
ROW_STEP = 2


@functools.lru_cache(maxsize=None)
def _fft_tables():
    t1 = np.arange(T1, dtype=np.int64)
    t2 = np.arange(N2, dtype=np.int64)
    f1 = np.arange(F1P, dtype=np.int64)
    keep = (f1 <= N1 // 2).astype(np.float64)
    k = (f1[None, :, None] * (N2 * t1[None, None, :] + t2[:, None, None])) % NFFT
    ph = 2.0 * np.pi * k.astype(np.float64) / NFFT
    fwd = np.concatenate([np.cos(ph) * keep[None, :, None],
                          -np.sin(ph) * keep[None, :, None]], axis=1)
    wgt = np.where((f1 == 0) | (f1 == N1 // 2), 1.0, 2.0) * keep / NFFT
    phq = np.transpose(ph, (0, 2, 1))
    inv = np.concatenate([np.cos(phq) * wgt[None, None, :],
                          -np.sin(phq) * wgt[None, None, :]], axis=2)
    a = np.arange(N2, dtype=np.int64)
    th = 2.0 * np.pi * ((a[:, None] * a[None, :]) % N2).astype(np.float64) / N2
    c, s = np.cos(th), np.sin(th)
    f2_fwd = np.block([[c, s], [-s, c]])
    f2_inv = np.block([[c, -s], [s, c]])
    pair = lambda m: np.concatenate([m[0::2], m[1::2]], axis=2)
    f32 = np.float32
    return pair(fwd).astype(f32), pair(inv).astype(f32), f2_fwd.astype(f32), f2_inv.astype(f32)


def _filter_constants():
    bands = np.linspace(1e-4, FILTER_BANDS - 1, FILTER_BANDS).astype(np.float32)
    max_decay = math.log(DECAY_TARGET) / FAST_DECAY_PCT
    min_decay = math.log(DECAY_TARGET) / SLOW_DECAY_PCT
    deltas = np.abs(np.linspace(min_decay, max_decay, D_HYENA)).astype(np.float32)
    return bands.reshape(FILTER_BANDS, 1), deltas.reshape(1, D_HYENA)


def _whole(shape):
    zeros = (0,) * len(shape)
    return pl.BlockSpec(shape, lambda *_: zeros, pipeline_mode=pl.Buffered(1))


def _rms(x, g):
    return x * lax.rsqrt(jnp.mean(x * x, axis=-1, keepdims=True) + EPS) * g


def _dot(a, b):
    return jnp.dot(a, b, preferred_element_type=F32)


def _sigmoid(x):
    return 0.5 * jnp.tanh(0.5 * x) + 0.5


def _silu(x):
    half = 0.5 * x
    return half * jnp.tanh(half) + half


def _swiglu_half_step(x, g_pre, wg_ref, wu_ref, wd_ref, g_post):
    h = _rms(x, g_pre).astype(BF16)
    gate = _dot(h, wg_ref[...])
    up = _dot(h, wu_ref[...])
    act = (_silu(gate) * up).astype(BF16)
    return x + 0.5 * _rms(_dot(act, wd_ref[...]), g_post)


def _block_diag(a, b):
    zero = jnp.zeros_like(a)
    return jnp.concatenate([jnp.concatenate([a, zero], axis=1),
                            jnp.concatenate([zero, b], axis=1)], axis=0)


def _t1_dft(load_signal, s_ref, fwd_ref):
    def body(p, carry):
        t2 = 2 * p
        sig = _block_diag(load_signal(t2).astype(BF16), load_signal(t2 + 1).astype(BF16))
        a = _dot(fwd_ref[p], sig)
        row0 = pl.multiple_of(t2 * PITCH_A, SUBLANES)
        s_ref[pl.ds(row0, 2 * F1P), :] = a[:, :CHUNK]
        s_ref[pl.ds(row0 + PITCH_A, 2 * F1P), :] = a[:, CHUNK:]
        return carry
    lax.fori_loop(0, N2 // 2, body, 0, unroll=T2_UNROLL // 2)


def _t2_major_group(s_ref, g):
    cols = []
    for k in range(F1_GROUP):
        f1 = g * F1_GROUP + k
        cols.append(jnp.concatenate([s_ref[pl.ds(f1, N2, stride=PITCH_A), :],
                                     s_ref[pl.ds(F1P + f1, N2, stride=PITCH_A), :]], axis=0).astype(BF16))
    return jnp.concatenate(cols, axis=1)


def _gather_t2_major(s_ref, r_ref):
    def body(g, carry):
        r_ref[g] = _t2_major_group(s_ref, g)
        return carry
    lax.fori_loop(0, N_GROUPS, body, 0)


def _forward_spectrum_rows(load_signal, s_ref, r_ref, fwd_ref):
    _t1_dft(load_signal, s_ref, fwd_ref)
    _gather_t2_major(s_ref, r_ref)


def _time_major_rows(w_ref):
    return lambda t2: w_ref[pl.ds(t2, T1, stride=PITCH_Y), :]


def _signal_rows(t1):
    return pl.ds(pl.multiple_of(t1 * PITCH_Y, SUBLANES), N2)


def _filter_kernel(w1t_ref, w1c_ref, w1s_ref, b1_ref, fr1_ref, w2_ref, b2_ref, fr2_ref,
                   w3f_ref, w3b_ref, bands_ref, delta_ref, fwd_ref, f2f_ref,
                   kr_ref, ki_ref, h_ref, wf_ref, wb_ref, s_ref, rf_ref, rb_ref):
    hi = lax.Precision.HIGHEST

    def dot_hi(a, b):
        return jnp.dot(a, b, preferred_element_type=F32, precision=hi)

    @pl.when(pl.program_id(0) == 0)
    def _():
        def body(i, carry):
            col0 = pl.multiple_of(i * FILTER_COLS, FILTER_COLS)
            n = (col0 + lax.broadcasted_iota(jnp.int32, (1, FILTER_COLS), 1)).astype(F32)
            t = n * (1.0 / (SEQ - 1))
            ang = bands_ref[...] * (n * (2.0 * math.pi / SEQ))
            zw = (w1t_ref[...] * t + dot_hi(w1c_ref[...], jnp.cos(ang))
                  - dot_hi(w1s_ref[...], jnp.sin(ang)))
            h = jnp.sin(fr1_ref[...] * (zw + b1_ref[...]))
            h = jnp.sin(fr2_ref[...] * (dot_hi(w2_ref[...], h) + b2_ref[...]))
            for k in range(FILTER_COLS // N2):
                h_ref[i * (FILTER_COLS // N2) + k] = h[:, k * N2:(k + 1) * N2]
            return carry
        lax.fori_loop(0, SEQ // FILTER_COLS, body, 0)

    w3_both = jnp.concatenate([w3f_ref[...], w3b_ref[...]], axis=1)
    delta_both = jnp.concatenate([delta_ref[...]] * 2, axis=1)

    def fill_body(t1, ss):
        row0 = pl.multiple_of(t1 * N2, N2)
        n = (row0 + lax.broadcasted_iota(jnp.int32, (N2, 1), 0)).astype(F32)
        decay = jnp.exp(-(n * (1.0 / (SEQ - 1))) * delta_both)
        rows = dot_hi(h_ref[t1].T, w3_both) * decay
        wf_ref[_signal_rows(t1), :] = rows[:, :CHUNK]
        wb_ref[_signal_rows(t1), :] = rows[:, CHUNK:]
        return ss + jnp.sum(rows * rows, axis=0, keepdims=True)
    ss = lax.fori_loop(0, T1, fill_body, jnp.zeros((1, 2 * CHUNK), F32), unroll=4)
    ss_f, ss_b = ss[:, :CHUNK], ss[:, CHUNK:]
    f0, b0 = wf_ref[0:1, :], wb_ref[0:1, :]
    _forward_spectrum_rows(_time_major_rows(wf_ref), s_ref, rf_ref, fwd_ref)
    _forward_spectrum_rows(_time_major_rows(wb_ref), s_ref, rb_ref, fwd_ref)
    scale = lax.rsqrt(ss_f + ss_b + 2.0 * f0 * b0 + EPS)
    scale = jnp.concatenate([scale] * F1_GROUP, axis=1)

    def body(g, carry):
        xf = _dot(f2f_ref[...], rf_ref[g])
        xb = _dot(f2f_ref[...], rb_ref[g])
        kr_ref[0, g] = scale * (xf[0:N2] + xb[0:N2])
        ki_ref[0, g] = scale * (xf[N2:2 * N2] - xb[N2:2 * N2])
        return carry
    lax.fori_loop(0, N_GROUPS, body, 0)


def _filter_spectrum(w1, b1, fr1, w2, b2, fr2, w3):
    fwd, _, f2f, _ = _fft_tables()
    bands, deltas = _filter_constants()
    col = lambda v: v.reshape(-1, 1)
    small = lambda shape: pl.BlockSpec(shape, lambda c: (0,) * len(shape))
    spec_out = pl.BlockSpec((1, N_GROUPS, N2, GROUP_LANES), lambda c: (c, 0, 0, 0))
    out_shape = jax.ShapeDtypeStruct((N_CHUNKS, N_GROUPS, N2, GROUP_LANES), F32)
    hid = FILTER_HIDDEN
    return pl.pallas_call(
        _filter_kernel,
        grid=(N_CHUNKS,),
        in_specs=[
            small((hid, 1)), small((hid, FILTER_BANDS)), small((hid, FILTER_BANDS)),
            small((hid, 1)), small((hid, 1)), small((hid, hid)), small((hid, 1)), small((hid, 1)),
            pl.BlockSpec((hid, CHUNK), lambda c: (0, c)),
            pl.BlockSpec((hid, CHUNK), lambda c: (0, N_CHUNKS + c)),
            small((FILTER_BANDS, 1)),
            pl.BlockSpec((1, CHUNK), lambda c: (0, c)),
            _whole((N2 // 2, 2 * F1P, 2 * T1)), _whole((2 * N2, 2 * N2)),
        ],
        out_specs=[spec_out, spec_out],
        out_shape=[out_shape, out_shape],
        scratch_shapes=[
            pltpu.VMEM((T1, hid, N2), F32),
            pltpu.VMEM((T1 * PITCH_Y, CHUNK), F32),
            pltpu.VMEM((T1 * PITCH_Y, CHUNK), F32),
            pltpu.VMEM((N2 * PITCH_A, CHUNK), F32),
            pltpu.VMEM((N_GROUPS, 2 * N2, GROUP_LANES), BF16),
            pltpu.VMEM((N_GROUPS, 2 * N2, GROUP_LANES), BF16),
        ],
        compiler_params=pltpu.CompilerParams(
            dimension_semantics=("arbitrary",), vmem_limit_bytes=VMEM_LIMIT_BYTES),
        name="filter_spectrum",
    )(col(w1[0]), w1[1:1 + FILTER_BANDS].T, w1[1 + FILTER_BANDS:].T, col(b1), col(fr1), w2.T,
      col(b2), col(fr2), w3, w3, jnp.asarray(bands), jnp.asarray(deltas),
      jnp.asarray(fwd).astype(BF16), jnp.asarray(f2f).astype(BF16))


def _ffn_proj_kernel(x_ref, g_pre_ref, wg_ref, wu_ref, wd_ref, g_post_ref, g_mix_ref, win_ref,
                     sw_ref, sb_ref, dww_ref, dwb_ref, lng_ref, lnb_ref,
                     x1_ref, sig_ref, x0_ref, cact_ref, hy_ext, glu_ext, hy_new, glu_new):
    tm = TOKEN_TILE
    hs, hc = SHORT_HALO, CONV_HALO
    step = pl.program_id(0)
    tiles_per_seq = SEQ // tm
    new_starts_seq = (step + tiles_per_seq - 1) % tiles_per_seq == 0
    cur_starts_seq = step % tiles_per_seq == 0

    def rows(start, size):
        return pl.ds(ROW_STEP * start, size, stride=ROW_STEP)

    def lanes(slab):
        return slice(slab * LANES, (slab + 1) * LANES)

    @pl.when(step == 0)
    def _():
        hy_ext[...] = jnp.zeros_like(hy_ext)
        glu_ext[...] = jnp.zeros_like(glu_ext)
        hy_new[...] = jnp.zeros_like(hy_new)
        glu_new[...] = jnp.zeros_like(glu_new)

    n_h = D_HYENA // LANES
    for r0 in range(0, tm, CONV_ROWS):
        def depthwise(ext, slab, halo, w_ref, b_ref, taps):
            acc = b_ref[:, lanes(slab)]
            for j in range(taps):
                acc = acc + (w_ref[j:j + 1, lanes(slab)]
                             * ext[slab, rows(r0 + halo + j - taps // 2, CONV_ROWS), :])
            return acc

        out_rows = slice(r0, r0 + CONV_ROWS)
        for slab in range(n_h):
            x0_ref[out_rows, lanes(slab)] = depthwise(hy_ext, slab, hs, sw_ref, sb_ref, HYENA_SHORT)
            sig_ref[out_rows, lanes(slab)] = (
                depthwise(hy_ext, 2 * n_h + slab, hs, sw_ref, sb_ref, HYENA_SHORT)
                * depthwise(hy_ext, n_h + slab, hs, sw_ref, sb_ref, HYENA_SHORT))
        c = jnp.concatenate([depthwise(glu_ext, slab, hc, dww_ref, dwb_ref, CONF_KERNEL)
                             for slab in range(D_CONV // LANES)], axis=1)
        mu = jnp.mean(c, axis=-1, keepdims=True)
        cc = c - mu
        c = cc * lax.rsqrt(jnp.mean(cc * cc, axis=-1, keepdims=True) + EPS) * lng_ref[...] + lnb_ref[...]
        cact_ref[out_rows, :] = _silu(c).astype(BF16)

    x1 = _swiglu_half_step(x_ref[...], g_pre_ref[...], wg_ref, wu_ref, wd_ref, g_post_ref[...])
    x1_ref[...] = x1
    u = _rms(x1, g_mix_ref[...]).astype(BF16)
    proj = _dot(u, win_ref[...])
    o1 = 3 * D_HYENA
    hy = proj[:, :o1]
    glu = proj[:, o1:o1 + D_CONV] * _sigmoid(proj[:, o1 + D_CONV:])

    for ext, new, cur, h in ((hy_ext, hy_new, hy, hs), (glu_ext, glu_new, glu, hc)):
        for slab in range(cur.shape[1] // LANES):
            tail = ext[slab, rows(tm, h), :]
            ext[slab, rows(0, h), :] = jnp.where(new_starts_seq, 0.0, tail)
            ext[slab, rows(h, tm), :] = new[:, lanes(slab)]
            ext[slab, rows(h + tm, h), :] = jnp.where(cur_starts_seq, 0.0, cur[0:h, lanes(slab)])
        new[...] = cur


def _ffn_proj(x, g_pre, wg, wu, wd, g_post, g_mix, win_a, short_w, short_b, dw_w, dw_b, ln_g, ln_b):
    tokens = x.shape[0]
    tm = TOKEN_TILE
    n = tokens // tm
    cur = lambda width: pl.BlockSpec((tm, width), lambda s: (jnp.minimum(s, n - 1), 0))
    lag = lambda width: pl.BlockSpec((tm, width), lambda s: (jnp.maximum(s - 2, 0), 0))
    n_a = win_a.shape[1]
    return pl.pallas_call(
        _ffn_proj_kernel,
        grid=(n + 2,),
        in_specs=[cur(D_MODEL), _whole((1, D_MODEL)), _whole((D_MODEL, D_FF)),
                  _whole((D_MODEL, D_FF)), _whole((D_FF, D_MODEL)), _whole((1, D_MODEL)),
                  _whole((1, D_MODEL)), _whole((D_MODEL, n_a)),
                  _whole((HYENA_SHORT, 3 * D_HYENA)), _whole((1, 3 * D_HYENA)),
                  _whole((CONF_KERNEL, D_CONV)), _whole((1, D_CONV)), _whole((1, D_CONV)),
                  _whole((1, D_CONV))],
        out_specs=[cur(D_MODEL), lag(D_HYENA), lag(D_HYENA), lag(D_CONV)],
        out_shape=[jax.ShapeDtypeStruct((tokens, D_MODEL), F32),
                   jax.ShapeDtypeStruct((tokens, D_HYENA), F32),
                   jax.ShapeDtypeStruct((tokens, D_HYENA), F32),
                   jax.ShapeDtypeStruct((tokens, D_CONV), BF16)],
        scratch_shapes=[pltpu.VMEM((3 * D_HYENA // LANES, ROW_STEP * (tm + 2 * SHORT_HALO), LANES), F32),
                        pltpu.VMEM((D_CONV // LANES, ROW_STEP * (tm + 2 * CONV_HALO), LANES), F32),
                        pltpu.VMEM((tm, 3 * D_HYENA), F32),
                        pltpu.VMEM((tm, D_CONV), F32)],
        compiler_params=pltpu.CompilerParams(
            dimension_semantics=("arbitrary",), vmem_limit_bytes=VMEM_LIMIT_BYTES),
        name="ffn_proj",
    )(x, g_pre, wg, wu, wd, g_post, g_mix, win_a, short_w, short_b, dw_w, dw_b, ln_g, ln_b)


def _long_conv_kernel(sig_ref, hb_ref, fwd_ref, inv_ref, f2f_ref, f2i_ref, kr_ref, ki_ref,
                      z_ref, s_ref, b_ref, w_ref):
    def in_body(t1, carry):
        w_ref[_signal_rows(t1), :] = sig_ref[0, pl.ds(pl.multiple_of(t1 * N2, N2), N2), :]
        return carry
    lax.fori_loop(0, T1, in_body, 0, unroll=4)
    _t1_dft(_time_major_rows(w_ref), s_ref, fwd_ref)

    for g in range(N_GROUPS):
        x = _dot(f2f_ref[...], _t2_major_group(s_ref, g))
        xr, xi = x[0:N2], x[N2:2 * N2]
        kr, ki = kr_ref[0, g], ki_ref[0, g]
        y = jnp.concatenate([xr * kr - xi * ki, xr * ki + xi * kr], axis=0).astype(BF16)
        b = _dot(f2i_ref[...], y)
        for k in range(F1_GROUP):
            row0 = (g * F1_GROUP + k) * PITCH_B
            b_ref[row0:row0 + 2 * N2, :] = b[:, k * CHUNK:(k + 1) * CHUNK]

    def spectrum_rows(t2):
        return jnp.concatenate([b_ref[pl.ds(t2, F1P, stride=PITCH_B), :],
                                b_ref[pl.ds(N2 + t2, F1P, stride=PITCH_B), :]], axis=0).astype(BF16)

    def inv_body(p, carry):
        t2 = 2 * p
        y = _dot(inv_ref[p], _block_diag(spectrum_rows(t2), spectrum_rows(t2 + 1)))
        w_ref[pl.ds(t2, T1, stride=PITCH_Y), :] = y[:, :CHUNK]
        w_ref[pl.ds(t2 + 1, T1, stride=PITCH_Y), :] = y[:, CHUNK:]
        return carry
    lax.fori_loop(0, N2 // 2, inv_body, 0, unroll=T2_UNROLL // 2)

    def out_body(t1, carry):
        rows = pl.ds(pl.multiple_of(t1 * N2, N2), N2)
        z_ref[0, rows, :] = w_ref[_signal_rows(t1), :] + sig_ref[0, rows, :] * hb_ref[...]
        return carry
    lax.fori_loop(0, T1, out_body, 0, unroll=4)


def _long_conv(sig, hy_bias, kr, ki):
    batch = sig.shape[0]
    fwd, inv, f2f, f2i = _fft_tables()
    spec_k = pl.BlockSpec((1, N_GROUPS, N2, GROUP_LANES), lambda c, b: (c, 0, 0, 0),
                          pipeline_mode=pl.Buffered(1))
    spec_io = pl.BlockSpec((1, SEQ, CHUNK), lambda c, b: (b, 0, c))
    return pl.pallas_call(
        _long_conv_kernel,
        grid=(N_CHUNKS, batch),
        in_specs=[spec_io, pl.BlockSpec((1, CHUNK), lambda c, b: (0, c)),
                  _whole((N2 // 2, 2 * F1P, 2 * T1)), _whole((N2 // 2, T1, 4 * F1P)),
                  _whole((2 * N2, 2 * N2)), _whole((2 * N2, 2 * N2)), spec_k, spec_k],
        out_specs=spec_io,
        out_shape=jax.ShapeDtypeStruct((batch, SEQ, D_HYENA), F32),
        scratch_shapes=[
            pltpu.VMEM((N2 * PITCH_A, CHUNK), F32),
            pltpu.VMEM((F1P * PITCH_B, CHUNK), F32),
            pltpu.VMEM((T1 * PITCH_Y, CHUNK), F32),
        ],
        compiler_params=pltpu.CompilerParams(
            dimension_semantics=("arbitrary", "arbitrary"), vmem_limit_bytes=VMEM_LIMIT_BYTES),
        name="long_conv",
    )(sig, hy_bias.reshape(1, -1),
      jnp.asarray(fwd).astype(BF16), jnp.asarray(inv).astype(BF16),
      jnp.asarray(f2f).astype(BF16), jnp.asarray(f2i).astype(BF16), kr, ki)


def _mix_ffn_kernel(x1_ref, z_ref, x0_ref, cact_ref, g_mix_ref, wgate_ref, hyw_ref, cvw_ref,
                    wout_ref, g_mixpost_ref, g_pre_ref, wg_ref, wu_ref, wd_ref, g_post_ref,
                    out_ref):
    y_a = _dot((z_ref[...] * x0_ref[...]).astype(BF16), hyw_ref[...])
    y_b = _dot(cact_ref[...], cvw_ref[...])
    x1 = x1_ref[...]
    u = _rms(x1, g_mix_ref[...]).astype(BF16)
    gates = _dot(u, wgate_ref[...])
    merged = _sigmoid(gates[:, :D_MODEL]) * y_a + _sigmoid(gates[:, D_MODEL:]) * y_b
    m = _dot(merged.astype(BF16), wout_ref[...])
    x2 = x1 + _rms(m, g_mixpost_ref[...])
    out_ref[...] = _swiglu_half_step(x2, g_pre_ref[...], wg_ref, wu_ref, wd_ref, g_post_ref[...])


def _mix_ffn(x1, z, x0, cact, g_mix, w_gates, hy_w_out, cv_w_out, w_out, g_mixpost,
             g_pre, wg, wu, wd, g_post):
    tokens = x1.shape[0]
    tm = TOKEN_TILE
    tile = lambda width: pl.BlockSpec((tm, width), lambda i: (i, 0))
    return pl.pallas_call(
        _mix_ffn_kernel,
        grid=(tokens // tm,),
        in_specs=[tile(D_MODEL), tile(D_HYENA), tile(D_HYENA), tile(D_CONV),
                  _whole((1, D_MODEL)), _whole((D_MODEL, 2 * D_MODEL)),
                  _whole((D_HYENA, D_MODEL)), _whole((D_CONV, D_MODEL)), _whole((D_MODEL, D_MODEL)),
                  _whole((1, D_MODEL)), _whole((1, D_MODEL)), _whole((D_MODEL, D_FF)),
                  _whole((D_MODEL, D_FF)), _whole((D_FF, D_MODEL)), _whole((1, D_MODEL))],
        out_specs=tile(D_MODEL),
        out_shape=jax.ShapeDtypeStruct((tokens, D_MODEL), F32),
        compiler_params=pltpu.CompilerParams(
            dimension_semantics=("arbitrary",), vmem_limit_bytes=VMEM_LIMIT_BYTES),
        name="mix_ffn",
    )(x1, z, x0, cact, g_mix, w_gates, hy_w_out, cv_w_out, w_out, g_mixpost, g_pre, wg, wu, wd, g_post)


def _encoder_layer(xs, p):
    row = lambda v: v.reshape(1, -1)
    bf = lambda w: w.astype(BF16)
    o2 = 3 * D_HYENA + 2 * D_CONV
    kr, ki = _filter_spectrum(p["hy_filt_w1"], p["hy_filt_b1"], p["hy_filt_freq1"], p["hy_filt_w2"],
                              p["hy_filt_b2"], p["hy_filt_freq2"], p["hy_filt_w3"])
    win_a = bf(p["w_in"][:, :o2])
    w_gates = bf(p["w_in"][:, o2:])
    ffn1 = (row(p["ffn1_norm_pre"]), bf(p["ffn1_w_gate"]), bf(p["ffn1_w_up"]), bf(p["ffn1_w_down"]),
            row(p["ffn1_norm_post"]))
    ffn2 = (row(p["ffn2_norm_pre"]), bf(p["ffn2_w_gate"]), bf(p["ffn2_w_up"]), bf(p["ffn2_w_down"]),
            row(p["ffn2_norm_post"]))
    g_mix = row(p["mix_norm_pre"])
    outs = []
    for x in xs:
        batch, seq, _ = x.shape
        assert seq == SEQ and (batch * seq) % TOKEN_TILE == 0
        x1, sig, x0, cact = _ffn_proj(
            x.reshape(batch * seq, D_MODEL), *ffn1, g_mix, win_a, p["hy_short_w"],
            row(p["hy_short_b"]), p["cv_dw_w"], row(p["cv_dw_b"]), row(p["cv_ln_g"]), row(p["cv_ln_b"]))
        z = _long_conv(sig.reshape(batch, seq, D_HYENA), p["hy_bias"], kr, ki)
        y = _mix_ffn(x1, z.reshape(batch * seq, D_HYENA), x0, cact, g_mix, w_gates,
                     bf(p["hy_w_out"]), bf(p["cv_w_out"]), bf(p["w_out"]), row(p["mix_norm_post"]), *ffn2)
        outs.append(y.reshape(batch, seq, D_MODEL))
    return outs


_PARAM_NAMES = (
    "ffn1_norm_pre", "ffn1_w_gate", "ffn1_w_up", "ffn1_w_down", "ffn1_norm_post",
    "mix_norm_pre", "w_in", "hy_short_w", "hy_short_b",
    "hy_filt_w1", "hy_filt_b1", "hy_filt_freq1", "hy_filt_w2", "hy_filt_b2", "hy_filt_freq2",
    "hy_filt_w3", "hy_bias", "hy_w_out", "cv_dw_w", "cv_dw_b", "cv_ln_g", "cv_ln_b", "cv_w_out",
    "w_out", "mix_norm_post",
    "ffn2_norm_pre", "ffn2_w_gate", "ffn2_w_up", "ffn2_w_down", "ffn2_norm_post")


def kernel(x_prompt, x_sample, ffn1_norm_pre, ffn1_w_gate, ffn1_w_up, ffn1_w_down, ffn1_norm_post, mix_norm_pre, w_in, hy_short_w, hy_short_b, hy_filt_w1, hy_filt_b1, hy_filt_freq1, hy_filt_w2, hy_filt_b2, hy_filt_freq2, hy_filt_w3, hy_bias, hy_w_out, cv_dw_w, cv_dw_b, cv_ln_g, cv_ln_b, cv_w_out, w_out, mix_norm_post, ffn2_norm_pre, ffn2_w_gate, ffn2_w_up, ffn2_w_down, ffn2_norm_post):
    stacked = dict(zip(_PARAM_NAMES, (
        ffn1_norm_pre, ffn1_w_gate, ffn1_w_up, ffn1_w_down, ffn1_norm_post,
        mix_norm_pre, w_in, hy_short_w, hy_short_b,
        hy_filt_w1, hy_filt_b1, hy_filt_freq1, hy_filt_w2, hy_filt_b2, hy_filt_freq2,
        hy_filt_w3, hy_bias, hy_w_out, cv_dw_w, cv_dw_b, cv_ln_g, cv_ln_b, cv_w_out,
        w_out, mix_norm_post,
        ffn2_norm_pre, ffn2_w_gate, ffn2_w_up, ffn2_w_down, ffn2_norm_post)))
    xs = [x_prompt, x_sample]
    for layer in range(ffn1_norm_pre.shape[0]):
        xs = _encoder_layer(xs, {k: v[layer] for k, v in stacked.items()})
    return (xs[0], xs[1])
```
